```python
import jax
import jax.numpy as jnp
from jax import lax
import numpy as np

D_MODEL = 1024
BATCH = 2
SEQ = 8192
DEPTH = 2
DEC_BATCH = 16
DEC_SEQ = 4096
PAST_LEN = 128

N_EVEN = (DEPTH + 1) // 2
N_ODD = DEPTH // 2

RET_HEADS = 4
RET_QK_DIM = 128
RET_V_DIM = 128
RET_CHUNK = 128
MLA_HEADS = 4
MLA_Q_RANK = 256
MLA_KV_RANK = 128
MLA_NOPE_DIM = 128
MLA_ROPE_DIM = 64
MLA_V_DIM = 128
MLA_QK_DIM = MLA_NOPE_DIM + MLA_ROPE_DIM
ATTN_Q_BLOCK = 128
ROPE_BASE = 10000.0
RET_QK_WIDTH = RET_HEADS * RET_QK_DIM
RET_V_WIDTH = RET_HEADS * RET_V_DIM
W_IN_COLS = 2 * RET_QK_WIDTH + 2 * RET_V_WIDTH + MLA_Q_RANK + MLA_KV_RANK + MLA_ROPE_DIM
MIX_WIDTH = RET_V_WIDTH + MLA_HEADS * MLA_V_DIM
CONV_WIDTH = 31
D_FF = -(-(8 * D_MODEL) // (3 * 256)) * 256
NORM_EPS = 1e-6

kernel_name = 'hybrid_retention_mla_conformer_encoder'


def rmsnorm(x, w):
    xf = x.astype(jnp.float32)
    y = xf * lax.rsqrt(jnp.mean(xf * xf, axis=-1, keepdims=True) + NORM_EPS)
    return (y * w.astype(jnp.float32)).astype(x.dtype)


def rope(x):
    s, d = x.shape[1], x.shape[-1]
    half = d // 2
    inv = ROPE_BASE ** (-jnp.arange(half, dtype=jnp.float32) / half)
    ang = jnp.arange(s, dtype=jnp.float32)[:, None] * inv[None, :]
    cos = jnp.cos(ang)[None, :, None, :]
    sin = jnp.sin(ang)[None, :, None, :]
    xf = x.astype(jnp.float32)
    x1, x2 = xf[..., :half], xf[..., half:]
    return jnp.concatenate([x1 * cos - x2 * sin, x1 * sin + x2 * cos], axis=-1).astype(x.dtype)


def retention_one_direction(q, k, v, log_g, strict):
    b, s, h, dk = q.shape
    dv = v.shape[-1]
    c = RET_CHUNK
    n = s // c
    qc = q.reshape(b, n, c, h, dk)
    kc = k.reshape(b, n, c, h, dk)
    vc = v.reshape(b, n, c, h, dv)
    pos = jnp.arange(c, dtype=jnp.float32)
    diff = pos[:, None] - pos[None, :]
    mask = (diff > 0) if strict else (diff >= 0)
    decay_intra = jnp.where(mask[None], jnp.exp(log_g[:, None, None] * jnp.maximum(diff, 0.0)[None]), 0.0)
    scores = jnp.einsum('bnihd,bnjhd->bnhij', qc, kc) * decay_intra[None, None]
    intra = jnp.einsum('bnhij,bnjhe->bnihe', scores, vc)
    k_dec = jnp.exp(log_g[None, :] * (c - 1.0 - pos)[:, None])
    kv = jnp.einsum('bnjhd,jh,bnjhe->nbhde', kc, k_dec, vc)
    g_chunk = jnp.exp(log_g * c)[None, :, None, None]

    def step(state, kv_n):
        return state * g_chunk + kv_n, state

    _, prev = lax.scan(step, jnp.zeros((b, h, dk, dv), jnp.float32), kv)
    q_dec = jnp.exp(log_g[None, :] * (pos + 1.0)[:, None])
    cross = jnp.einsum('bnihd,ih,nbhde->bnihe', qc, q_dec, prev)
    return (intra + cross).reshape(b, s, h, dv)


def blocked_attention(q, k, v):
    b, s, h, d = q.shape
    e = v.shape[-1]
    nb = s // ATTN_Q_BLOCK
    scale = d ** -0.5
    qb = jnp.moveaxis(q.reshape(b, nb, ATTN_Q_BLOCK, h, d), 1, 0)

    def one_block(q_blk):
        sc = jnp.einsum('bqhd,bkhd->bhqk', q_blk, k, preferred_element_type=jnp.float32) * scale
        p = jax.nn.softmax(sc, axis=-1)
        return jnp.einsum('bhqk,bkhe->bqhe', p.astype(v.dtype), v)

    o = lax.map(one_block, qb)
    return jnp.moveaxis(o, 0, 1).reshape(b, s, h * e)


def hybrid_layer(x, norm_w, w_in, decay_fwd, decay_bwd, gn_w, w_q_b, q_a_norm, w_kv_b, kv_a_norm,
                 q_norm, k_norm, w_out):
    b, s, _ = x.shape
    hcat = rmsnorm(x, norm_w) @ w_in
    sizes = [RET_QK_WIDTH, RET_QK_WIDTH, RET_V_WIDTH, RET_V_WIDTH, MLA_Q_RANK, MLA_KV_RANK, MLA_ROPE_DIM]
    idx = np.cumsum(sizes)[:-1].tolist()
    q_r, k_r, v_r, g_r, c_q, c_kv, k_pe = jnp.split(hcat, idx, axis=-1)

    q_r = rope(q_r.reshape(b, s, RET_HEADS, RET_QK_DIM)).astype(jnp.float32) * (RET_QK_DIM ** -0.5)
    k_r = rope(k_r.reshape(b, s, RET_HEADS, RET_QK_DIM)).astype(jnp.float32)
    v_r = v_r.reshape(b, s, RET_HEADS, RET_V_DIM).astype(jnp.float32)
    lf = -jnp.exp(decay_fwd.astype(jnp.float32))
    lb = -jnp.exp(decay_bwd.astype(jnp.float32))
    fwd = retention_one_direction(q_r, k_r, v_r, lf, False)
    bwd = jnp.flip(retention_one_direction(jnp.flip(q_r, 1), jnp.flip(k_r, 1), jnp.flip(v_r, 1), lb, True), 1)
    ret = rmsnorm(fwd + bwd, gn_w)
    ret = ret.reshape(b, s, RET_V_WIDTH).astype(x.dtype) * jax.nn.silu(g_r)

    q = (rmsnorm(c_q, q_a_norm) @ w_q_b).reshape(b, s, MLA_HEADS, MLA_QK_DIM)
    kv = (rmsnorm(c_kv, kv_a_norm) @ w_kv_b).reshape(b, s, MLA_HEADS, MLA_NOPE_DIM + MLA_V_DIM)
    k_nope, v = kv[..., :MLA_NOPE_DIM], kv[..., MLA_NOPE_DIM:]
    q = jnp.concatenate([q[..., :MLA_NOPE_DIM], rope(q[..., MLA_NOPE_DIM:])], axis=-1)
    k_rot = jnp.broadcast_to(rope(k_pe[:, :, None, :]), (b, s, MLA_HEADS, MLA_ROPE_DIM))
    k = jnp.concatenate([k_nope, k_rot], axis=-1)
    q = rmsnorm(q, q_norm)
    k = rmsnorm(k, k_norm)
    att = blocked_attention(q, k, v).astype(x.dtype)

    mix = jnp.concatenate([ret, att], axis=-1) @ w_out
    return x + mix


def conv_layer(x, norm_w, w_pw1, b_pw1, w_dw, b_dw, mid_norm_w, w_pw2, b_pw2):
    h = rmsnorm(x, norm_w) @ w_pw1 + b_pw1
    h = jax.nn.glu(h, axis=-1)
    h = lax.conv_general_dilated(h, w_dw[:, None, :], window_strides=(1,),
                                 padding=[((CONV_WIDTH - 1) // 2, (CONV_WIDTH - 1) // 2)],
                                 dimension_numbers=('NWC', 'WIO', 'NWC'),
                                 feature_group_count=D_MODEL) + b_dw
    h = jax.nn.silu(rmsnorm(h, mid_norm_w))
    return x + h @ w_pw2 + b_pw2


def swiglu_ffn(x, norm_w, w_gate, w_up, w_down):
    h = rmsnorm(x, norm_w)
    return x + (jax.nn.silu(h @ w_gate) * (h @ w_up)) @ w_down


def setup_inputs(seed: int = 0) -> dict:
    key = jax.random.key(seed)
    ks = jax.random.split(key, 40)
    counter = [0]

    def nk():
        counter[0] += 1
        return ks[counter[0] - 1]

    def w(shape, fan_in):
        return jax.random.normal(nk(), shape, jnp.float32) * (fan_in ** -0.5)

    def gain(shape):
        return 1.0 + 0.02 * jax.random.normal(nk(), shape, jnp.float32)

    def bias(shape):
        return 0.01 * jax.random.normal(nk(), shape, jnp.float32)

    base = jnp.log(-jnp.log(1.0 - 2.0 ** (-5.0 - jnp.arange(RET_HEADS, dtype=jnp.float32))))

    def decay():
        return base[None, :] + 0.05 * jax.random.normal(nk(), (N_EVEN, RET_HEADS), jnp.float32)

    return {
        'x_prompt': jax.random.normal(nk(), (BATCH, SEQ, D_MODEL), jnp.float32),
        'x_sample': jax.random.normal(nk(), (DEC_BATCH, DEC_SEQ, D_MODEL), jnp.float32),
        'hyb_norm_w': gain((N_EVEN, D_MODEL)),
        'hyb_w_in': w((N_EVEN, D_MODEL, W_IN_COLS), D_MODEL),
        'ret_decay_fwd': decay(),
        'ret_decay_bwd': decay(),
        'ret_gn_w': gain((N_EVEN, RET_HEADS, RET_V_DIM)),
        'mla_w_q_b': w((N_EVEN, MLA_Q_RANK, MLA_HEADS * MLA_QK_DIM), MLA_Q_RANK),
        'mla_q_a_norm': gain((N_EVEN, MLA_Q_RANK)),
        'mla_w_kv_b': w((N_EVEN, MLA_KV_RANK, MLA_HEADS * (MLA_NOPE_DIM + MLA_V_DIM)), MLA_KV_RANK),
        'mla_kv_a_norm': gain((N_EVEN, MLA_KV_RANK)),
        'mla_q_norm': gain((N_EVEN, MLA_QK_DIM)),
        'mla_k_norm': gain((N_EVEN, MLA_QK_DIM)),
        'hyb_w_out': w((N_EVEN, MIX_WIDTH, D_MODEL), MIX_WIDTH),
        'conv_norm_w': gain((N_ODD, D_MODEL)),
        'conv_w_pw1': w((N_ODD, D_MODEL, 2 * D_MODEL), D_MODEL),
        'conv_b_pw1': bias((N_ODD, 2 * D_MODEL)),
        'conv_w_dw': w((N_ODD, CONV_WIDTH, D_MODEL), CONV_WIDTH),
        'conv_b_dw': bias((N_ODD, D_MODEL)),
        'conv_mid_norm_w': gain((N_ODD, D_MODEL)),
        'conv_w_pw2': w((N_ODD, D_MODEL, D_MODEL), D_MODEL),
        'conv_b_pw2': bias((N_ODD, D_MODEL)),
        'ffn_norm_w': gain((DEPTH, D_MODEL)),
        'ffn_w_gate': w((DEPTH, D_MODEL, D_FF), D_MODEL),
        'ffn_w_up': w((DEPTH, D_MODEL, D_FF), D_MODEL),
        'ffn_w_down': w((DEPTH, D_FF, D_MODEL), D_FF),
    }


def reference(x_prompt, x_sample, hyb_norm_w, hyb_w_in, ret_decay_fwd, ret_decay_bwd, ret_gn_w,
              mla_w_q_b, mla_q_a_norm, mla_w_kv_b, mla_kv_a_norm, mla_q_norm, mla_k_norm, hyb_w_out,
              conv_norm_w, conv_w_pw1, conv_b_pw1, conv_w_dw, conv_b_dw, conv_mid_norm_w, conv_w_pw2,
              conv_b_pw2, ffn_norm_w, ffn_w_gate, ffn_w_up, ffn_w_down):
    def run(x):
        for layer in range(DEPTH):
            i = layer // 2
            if layer % 2 == 0:
                x = hybrid_layer(x, hyb_norm_w[i], hyb_w_in[i], ret_decay_fwd[i], ret_decay_bwd[i],
                                 ret_gn_w[i], mla_w_q_b[i], mla_q_a_norm[i], mla_w_kv_b[i],
                                 mla_kv_a_norm[i], mla_q_norm[i], mla_k_norm[i], hyb_w_out[i])
            else:
                x = conv_layer(x, conv_norm_w[i], conv_w_pw1[i], conv_b_pw1[i], conv_w_dw[i],
                               conv_b_dw[i], conv_mid_norm_w[i], conv_w_pw2[i], conv_b_pw2[i])
            x = swiglu_ffn(x, ffn_norm_w[layer], ffn_w_gate[layer], ffn_w_up[layer], ffn_w_down[layer])
        return x

    y_prompt = run(x_prompt)
    y_sample = run(x_sample)
    return (y_prompt, y_sample)
```

```python
import functools
import math

import jax
import jax.numpy as jnp
from jax import lax
from jax.experimental import pallas as pl
from jax.experimental.pallas import tpu as pltpu

D_MODEL = 1024
RET_HEADS = 4
RET_DIM = 128
RET_WIDTH = RET_HEADS * RET_DIM
MLA_HEADS = 4
MLA_Q_RANK = 256
MLA_KV_RANK = 128
MLA_NOPE = 128
MLA_ROPE = 64
MLA_V = 128
MLA_QK = MLA_NOPE + MLA_ROPE
MLA_QK_PAD = 256
ROPE_BASE = 10000.0
CONV_WIDTH = 31
CONV_PAD = (CONV_WIDTH - 1) // 2
HALO = 16
D_FF = 2816
NORM_EPS = 1e-6
LOG2E = math.log2(math.e)

LANE = 128
SUBLANE = 8
VMEM_LIMIT = 56 * 1024 * 1024

F32 = jnp.float32
BF16 = jnp.bfloat16


def _dot(a, b):
    return jnp.dot(a, b, preferred_element_type=F32)


def _dot_nt(a, b):
    return lax.dot_general(a, b, (((1,), (1,)), ((), ())), preferred_element_type=F32)


def _dot_tn(a, b):
    return lax.dot_general(a, b, (((0,), (0,)), ((), ())), preferred_element_type=F32)


def _rms(x, w):
    return x * lax.rsqrt(jnp.mean(x * x, axis=-1, keepdims=True) + NORM_EPS) * w


def _silu(g):
    return g / (1.0 + jnp.exp(-g))


def _const_spec(shape):
    return pl.BlockSpec(shape, lambda *_: (0,) * len(shape), pipeline_mode=pl.Buffered(1))


def _hyb_in_kernel(x_ref, nw_ref, win_ref, cr_ref, sr_ref, cm_ref, sm_ref, qan_ref, wqb_ref, kvan_ref,
                   wkvb_ref, qn_ref, kn_ref, qr_ref, kr_ref, vr_ref, g_ref, q_ref, k_ref, v_ref):
    h = _rms(x_ref[...], nw_ref[...]).astype(BF16)
    cr, sr = cr_ref[...], sr_ref[...]
    cm, sm = cm_ref[...], sm_ref[...]

    qk = _dot(h, win_ref[:, 0:2 * RET_WIDTH])
    for i in range(2 * RET_HEADS):
        seg = qk[:, i * LANE:(i + 1) * LANE]
        rot = seg * cr + pltpu.roll(seg, RET_DIM // 2, 1) * sr
        if i < RET_HEADS:
            qr_ref[:, i * LANE:(i + 1) * LANE] = (rot * (RET_DIM ** -0.5)).astype(BF16)
        else:
            j = i - RET_HEADS
            kr_ref[:, j * LANE:(j + 1) * LANE] = rot.astype(BF16)

    vg = _dot(h, win_ref[:, 2 * RET_WIDTH:4 * RET_WIDTH])
    vr_ref[...] = vg[:, :RET_WIDTH].astype(BF16)
    g_ref[...] = vg[:, RET_WIDTH:].astype(BF16)

    lat = _dot(h, win_ref[:, 4 * RET_WIDTH:4 * RET_WIDTH + 512])
    cq = _rms(lat[:, :MLA_Q_RANK], qan_ref[...]).astype(BF16)
    q4 = _dot(cq, wqb_ref[...])
    ckv = _rms(lat[:, MLA_Q_RANK:MLA_Q_RANK + MLA_KV_RANK], kvan_ref[...]).astype(BF16)
    kv4 = _dot(ckv, wkvb_ref[...])

    def rope64(t):
        return t * cm + pltpu.roll(t, MLA_ROPE // 2, 1) * sm

    krot = rope64(lat[:, MLA_Q_RANK + MLA_KV_RANK:])
    krot_ss = jnp.sum(krot * krot, axis=-1, keepdims=True)
    qn, kn = qn_ref[...], kn_ref[...]
    for hh in range(MLA_HEADS):
        base = hh * MLA_QK_PAD
        nope = q4[:, base:base + LANE]
        rot = rope64(q4[:, base + LANE:base + 2 * LANE])
        ss = jnp.sum(nope * nope, axis=-1, keepdims=True) + jnp.sum(rot * rot, axis=-1, keepdims=True)
        r = lax.rsqrt(ss * (1.0 / MLA_QK) + NORM_EPS) * (MLA_QK ** -0.5 * LOG2E)
        q_ref[:, base:base + LANE] = (nope * r * qn[:, :LANE]).astype(BF16)
        q_ref[:, base + LANE:base + 2 * LANE] = (rot * r * qn[:, LANE:]).astype(BF16)

        knope = kv4[:, base:base + LANE]
        rk = lax.rsqrt((jnp.sum(knope * knope, axis=-1, keepdims=True) + krot_ss) * (1.0 / MLA_QK) + NORM_EPS)
        k_ref[:, base:base + LANE] = (knope * rk * kn[:, :LANE]).astype(BF16)
        k_ref[:, base + LANE:base + 2 * LANE] = (krot * rk * kn[:, LANE:]).astype(BF16)
        v_ref[:, hh * LANE:(hh + 1) * LANE] = kv4[:, base + LANE:base + 2 * LANE].astype(BF16)


def _hyb_in(x, p, tabs, seq, tm):
    t = x.shape[0]
    nps = seq // tm
    row = lambda w: pl.BlockSpec((tm, w), lambda i: (i, 0))
    tab = pl.BlockSpec((tm, LANE), lambda i: (i % nps, 0))
    out_w = (RET_WIDTH, RET_WIDTH, RET_WIDTH, RET_WIDTH, MLA_HEADS * MLA_QK_PAD, MLA_HEADS * MLA_QK_PAD,
             MLA_HEADS * MLA_V)
    return pl.pallas_call(
        _hyb_in_kernel,
        grid=(t // tm,),
        in_specs=[row(D_MODEL), _const_spec((1, D_MODEL)), _const_spec(p['w_in'].shape), tab, tab, tab, tab,
                  _const_spec((1, MLA_Q_RANK)), _const_spec(p['w_q_b'].shape), _const_spec((1, MLA_KV_RANK)),
                  _const_spec(p['w_kv_b'].shape), _const_spec((1, MLA_QK_PAD)), _const_spec((1, MLA_QK_PAD))],
        out_specs=[row(w) for w in out_w],
        out_shape=[jax.ShapeDtypeStruct((t, w), BF16) for w in out_w],
        compiler_params=pltpu.CompilerParams(dimension_semantics=("parallel",), vmem_limit_bytes=VMEM_LIMIT),
        name="hyb_in",
    )(x, p['hyb_norm_w'], p['w_in'], tabs['cr'], tabs['sr'], tabs['cm'], tabs['sm'], p['q_a_norm'], p['w_q_b'],
      p['kv_a_norm'], p['w_kv_b'], p['q_norm'], p['k_norm'])


def _ret_kernel(q_ref, k_ref, v_ref, g_ref, df_ref, db_ref, gn_ref, o_ref, acc_ref, dmat_ref, tab_ref, *,
                chunk, seq):
    c = chunk
    n = seq // c
    lf = -jnp.exp(df_ref[...])
    lb = -jnp.exp(db_ref[...])
    row = lax.broadcasted_iota(jnp.int32, (c, c), 0).astype(F32)
    col = lax.broadcasted_iota(jnp.int32, (c, c), 1).astype(F32)
    diff = row - col
    dmat_ref[...] = jnp.where(diff >= 0, jnp.exp(lf * jnp.maximum(diff, 0.0)),
                              jnp.exp(lb * jnp.maximum(-diff, 0.0)))
    pos = lax.broadcasted_iota(jnp.int32, (c, LANE), 0).astype(F32)
    lf1, lb1 = lf[:, :LANE], lb[:, :LANE]
    tab_ref[0] = jnp.exp(lf1 * (pos + 1.0))
    tab_ref[1] = jnp.exp(lf1 * (c - 1.0 - pos))
    tab_ref[2] = jnp.exp(lb1 * (c - pos))
    tab_ref[3] = jnp.exp(lb1 * pos)
    gf = jnp.exp(lf1 * float(c))
    gb = jnp.exp(lb1 * float(c))
    zero_state = jnp.zeros((RET_DIM, RET_DIM), F32)

    def fwd(i, sf):
        rows = pl.ds(pl.multiple_of(i * c, c), c)
        qc, kc, vc = q_ref[rows, :], k_ref[rows, :], v_ref[rows, :]
        a = (_dot_nt(qc, kc) * dmat_ref[...]).astype(BF16)
        acc_ref[rows, :] = _dot(a, vc) + tab_ref[0] * _dot(qc, sf.astype(BF16))
        kv = _dot_tn(kc, (vc.astype(F32) * tab_ref[1]).astype(BF16))
        return sf * gf + kv

    lax.fori_loop(0, n, fwd, zero_state)

    def bwd(j, sb):
        rows = pl.ds(pl.multiple_of((n - 1 - j) * c, c), c)
        qc, kc, vc = q_ref[rows, :], k_ref[rows, :], v_ref[rows, :]
        o = acc_ref[rows, :] + tab_ref[2] * _dot(qc, sb.astype(BF16))
        y = _rms(o, gn_ref[...])
        o_ref[rows, :] = (y * _silu(g_ref[rows, :].astype(F32))).astype(BF16)
        kv = _dot_tn(kc, (vc.astype(F32) * tab_ref[3]).astype(BF16))
        return sb * gb + kv

    lax.fori_loop(0, n, bwd, zero_state)


def _retention(qr, kr, vr, g, p, chunk):
    b, seq, _ = qr.shape
    head = pl.BlockSpec((None, seq, RET_DIM), lambda bi, hi: (bi, 0, hi))
    par = lambda w: pl.BlockSpec((None, 1, w), lambda bi, hi: (hi, 0, 0))
    return pl.pallas_call(
        functools.partial(_ret_kernel, chunk=chunk, seq=seq),
        grid=(b, RET_HEADS),
        in_specs=[head, head, head, head, par(chunk), par(chunk), par(RET_DIM)],
        out_specs=head,
        out_shape=jax.ShapeDtypeStruct((b, seq, RET_WIDTH), BF16),
        scratch_shapes=[pltpu.VMEM((seq, RET_DIM), F32), pltpu.VMEM((chunk, chunk), F32),
                        pltpu.VMEM((4, chunk, LANE), F32)],
        compiler_params=pltpu.CompilerParams(dimension_semantics=("parallel", "parallel"),
                                             vmem_limit_bytes=VMEM_LIMIT),
        name="retention",
    )(qr, kr, vr, g, p['decay_fwd'], p['decay_bwd'], p['gn_w'])


def _attn_kernel(q_ref, k_ref, v_ref, o_ref, m_ref, l_ref, acc_ref, *, tk, seq):
    q = q_ref[...]
    m_ref[...] = jnp.full(m_ref.shape, -jnp.inf, F32)
    l_ref[...] = jnp.zeros(l_ref.shape, F32)
    acc_ref[...] = jnp.zeros(acc_ref.shape, F32)

    def body(j, carry):
        rows = pl.ds(pl.multiple_of(j * tk, tk), tk)
        s = _dot_nt(q, k_ref[rows, :])
        m_prev = m_ref[...]
        m_new = jnp.maximum(m_prev, jnp.max(s, axis=1, keepdims=True))
        alpha = jnp.exp2(m_prev - m_new)
        p = jnp.exp2(s - jnp.concatenate([m_new] * (tk // LANE), axis=1))
        l_ref[...] = alpha * l_ref[...] + jnp.sum(p, axis=1, keepdims=True)
        acc_ref[...] = acc_ref[...] * alpha + _dot(p.astype(BF16), v_ref[rows, :])
        m_ref[...] = m_new
        return carry

    lax.fori_loop(0, seq // tk, body, 0)
    o_ref[...] = (acc_ref[...] / l_ref[...]).astype(BF16)


def _attention(q, k, v, tq, tk):
    b, seq, _ = q.shape
    return pl.pallas_call(
        functools.partial(_attn_kernel, tk=tk, seq=seq),
        grid=(b, MLA_HEADS, seq // tq),
        in_specs=[pl.BlockSpec((None, tq, MLA_QK_PAD), lambda bi, hi, qi: (bi, qi, hi)),
                  pl.BlockSpec((None, seq, MLA_QK_PAD), lambda bi, hi, qi: (bi, 0, hi)),
                  pl.BlockSpec((None, seq, MLA_V), lambda bi, hi, qi: (bi, 0, hi))],
        out_specs=pl.BlockSpec((None, tq, MLA_V), lambda bi, hi, qi: (bi, qi, hi)),
        out_shape=jax.ShapeDtypeStruct((b, seq, MLA_HEADS * MLA_V), BF16),
        scratch_shapes=[pltpu.VMEM((tq, LANE), F32), pltpu.VMEM((tq, LANE), F32), pltpu.VMEM((tq, MLA_V), F32)],
        compiler_params=pltpu.CompilerParams(dimension_semantics=("parallel", "parallel", "arbitrary"),
                                             vmem_limit_bytes=VMEM_LIMIT),
        name="attention",
    )(q, k, v)


FF_CHUNK = 256


def _ffn(x1, nw_ref, wg_ref, wu_ref, wd_ref, act_ref):
    h = _rms(x1, nw_ref[...]).astype(BF16)
    for c0 in range(0, D_FF, FF_CHUNK):
        g = _dot(h, wg_ref[:, c0:c0 + FF_CHUNK])
        u = _dot(h, wu_ref[:, c0:c0 + FF_CHUNK])
        act_ref[:, c0:c0 + FF_CHUNK] = (_silu(g) * u).astype(BF16)
    return x1 + _dot(act_ref[...], wd_ref[...])


def _ffn_specs():
    return [_const_spec((1, D_MODEL)), _const_spec((D_MODEL, D_FF)), _const_spec((D_MODEL, D_FF)),
            _const_spec((D_FF, D_MODEL))]


def _mix_ffn_kernel(x_ref, ret_ref, att_ref, wor_ref, woa_ref, nw_ref, wg_ref, wu_ref, wd_ref, o_ref, act_ref):
    x1 = x_ref[...] + _dot(ret_ref[...], wor_ref[...]) + _dot(att_ref[...], woa_ref[...])
    o_ref[...] = _ffn(x1, nw_ref, wg_ref, wu_ref, wd_ref, act_ref)


def _mix_ffn(x, ret, att, p, f, tm):
    t = x.shape[0]
    row = lambda w: pl.BlockSpec((tm, w), lambda i: (i, 0))
    return pl.pallas_call(
        _mix_ffn_kernel,
        grid=(t // tm,),
        in_specs=[row(D_MODEL), row(RET_WIDTH), row(MLA_HEADS * MLA_V), _const_spec((RET_WIDTH, D_MODEL)),
                  _const_spec((MLA_HEADS * MLA_V, D_MODEL))] + _ffn_specs(),
        out_specs=row(D_MODEL),
        out_shape=jax.ShapeDtypeStruct((t, D_MODEL), F32),
        scratch_shapes=[pltpu.VMEM((tm, D_FF), BF16)],
        compiler_params=pltpu.CompilerParams(dimension_semantics=("parallel",), vmem_limit_bytes=VMEM_LIMIT),
        name="mix_ffn",
    )(x, ret, att, p['w_out_ret'], p['w_out_att'], f['norm_w'], f['w_gate'], f['w_up'], f['w_down'])


CONV_ROWS = 64


def _conv_ffn_kernel(xp_ref, x_ref, xn_ref, nw_ref, w1_ref, b1_ref, wdw_ref, bdw_ref, mw_ref, w2_ref, b2_ref,
                     fnw_ref, wg_ref, wu_ref, wd_ref, o_ref, xcat_ref, hbuf_ref, cbuf_ref, act_ref, *, tm, nps):
    i = pl.program_id(0)
    first = (i % nps) == 0
    last = (i % nps) == nps - 1
    x = x_ref[...]
    nw = nw_ref[...]
    xcat_ref[0:HALO, :] = _rms(xp_ref[...], nw).astype(BF16)
    xcat_ref[HALO:HALO + tm, :] = _rms(x, nw).astype(BF16)
    xcat_ref[HALO + tm:, :] = _rms(xn_ref[...], nw).astype(BF16)
    hcat = xcat_ref[...]
    a = _dot(hcat, w1_ref[:, :D_MODEL]) + b1_ref[:, :D_MODEL]
    gate = _dot(hcat, w1_ref[:, D_MODEL:]) + b1_ref[:, D_MODEL:]
    hbuf_ref[...] = a / (1.0 + jnp.exp(-gate))

    @pl.when(first)
    def _():
        hbuf_ref[0:HALO, :] = jnp.zeros((HALO, D_MODEL), F32)

    @pl.when(last)
    def _():
        hbuf_ref[HALO + tm:, :] = jnp.zeros((HALO, D_MODEL), F32)

    def conv_block(r, carry):
        r0 = pl.multiple_of(r * CONV_ROWS, CONV_ROWS)
        for lb in range(D_MODEL // LANE):
            lanes = slice(lb * LANE, (lb + 1) * LANE)
            acc = None
            for sh in range(SUBLANE):
                part = None
                for kk in range(CONV_WIDTH):
                    off = HALO - CONV_PAD + kk
                    if off % SUBLANE != sh:
                        continue
                    al = off - sh
                    t = hbuf_ref[pl.ds(r0 + al, CONV_ROWS + SUBLANE), lanes] * wdw_ref[kk:kk + 1, lanes]
                    part = t if part is None else part + t
                part = part[sh:sh + CONV_ROWS, :]
                acc = part if acc is None else acc + part
            cbuf_ref[pl.ds(r0, CONV_ROWS), lanes] = acc
        return carry

    lax.fori_loop(0, tm // CONV_ROWS, conv_block, 0)

    h2 = _silu(_rms(cbuf_ref[...] + bdw_ref[...], mw_ref[...])).astype(BF16)
    x1 = x + _dot(h2, w2_ref[...]) + b2_ref[...]
    o_ref[...] = _ffn(x1, fnw_ref, wg_ref, wu_ref, wd_ref, act_ref)


def _conv_ffn(x, p, f, seq, tm):
    t = x.shape[0]
    nps = seq // tm
    hb = tm // HALO
    x3 = x.reshape(t // HALO, HALO, D_MODEL)
    nblk = t // HALO
    return pl.pallas_call(
        functools.partial(_conv_ffn_kernel, tm=tm, nps=nps),
        grid=(t // tm,),
        in_specs=[pl.BlockSpec((None, HALO, D_MODEL), lambda i: (jnp.maximum(i * hb - 1, 0), 0, 0)),
                  pl.BlockSpec((tm, D_MODEL), lambda i: (i, 0)),
                  pl.BlockSpec((None, HALO, D_MODEL), lambda i: (jnp.minimum((i + 1) * hb, nblk - 1), 0, 0)),
                  _const_spec((1, D_MODEL)), _const_spec((D_MODEL, 2 * D_MODEL)), _const_spec((1, 2 * D_MODEL)),
                  _const_spec((CONV_WIDTH, D_MODEL)), _const_spec((1, D_MODEL)), _const_spec((1, D_MODEL)),
                  _const_spec((D_MODEL, D_MODEL)), _const_spec((1, D_MODEL))] + _ffn_specs(),
        out_specs=pl.BlockSpec((tm, D_MODEL), lambda i: (i, 0)),
        out_shape=jax.ShapeDtypeStruct((t, D_MODEL), F32),
        scratch_shapes=[pltpu.VMEM((tm + 2 * HALO, D_MODEL), BF16), pltpu.VMEM((tm + 2 * HALO, D_MODEL), F32),
                        pltpu.VMEM((tm, D_MODEL), F32), pltpu.VMEM((tm, D_FF), BF16)],
        compiler_params=pltpu.CompilerParams(dimension_semantics=("parallel",), vmem_limit_bytes=VMEM_LIMIT),
        name="conv_ffn",
    )(x3, x, x3, p['norm_w'], p['w_pw1'], p['b_pw1'], p['w_dw'], p['b_dw'], p['mid_norm_w'], p['w_pw2'],
      p['b_pw2'], f['norm_w'], f['w_gate'], f['w_up'], f['w_down'])


def _rope_tables(seq):
    def cos_sin(half):
        inv = ROPE_BASE ** (-jnp.arange(half, dtype=F32) / half)
        ang = jnp.arange(seq, dtype=F32)[:, None] * inv[None, :]
        return jnp.cos(ang), jnp.sin(ang)

    c, s = cos_sin(RET_DIM // 2)
    cm, sm = cos_sin(MLA_ROPE // 2)
    z = jnp.zeros_like(cm)
    return {'cr': jnp.concatenate([c, c], axis=1), 'sr': jnp.concatenate([-s, s], axis=1),
            'cm': jnp.concatenate([cm, cm, z, z], axis=1), 'sm': jnp.concatenate([-sm, sm, z, z], axis=1)}


def _prep_hybrid(norm_w, w_in, decay_fwd, decay_bwd, gn_w, w_q_b, q_a_norm, w_kv_b, kv_a_norm, q_norm, k_norm,
                 w_out, chunk):
    kpe = w_in[:, -MLA_ROPE:]
    wq = w_q_b.reshape(MLA_Q_RANK, MLA_HEADS, MLA_QK)
    wq = jnp.concatenate([wq, wq[:, :, MLA_NOPE:]], axis=-1).reshape(MLA_Q_RANK, MLA_HEADS * MLA_QK_PAD)
    pad = lambda w: jnp.concatenate([w, jnp.zeros((MLA_QK_PAD - MLA_QK,), F32)])[None, :]
    bc = lambda d: jnp.broadcast_to(d[:, None, None], (RET_HEADS, 1, chunk))
    return {
        'hyb_norm_w': norm_w[None, :], 'w_in': jnp.concatenate([w_in, kpe], axis=1).astype(BF16),
        'decay_fwd': bc(decay_fwd), 'decay_bwd': bc(decay_bwd), 'gn_w': gn_w[:, None, :],
        'w_q_b': wq.astype(BF16), 'q_a_norm': q_a_norm[None, :], 'w_kv_b': w_kv_b.astype(BF16),
        'kv_a_norm': kv_a_norm[None, :], 'q_norm': pad(q_norm), 'k_norm': pad(k_norm),
        'w_out_ret': w_out[:RET_WIDTH].astype(BF16), 'w_out_att': w_out[RET_WIDTH:].astype(BF16),
    }


def _prep_conv(norm_w, w_pw1, b_pw1, w_dw, b_dw, mid_norm_w, w_pw2, b_pw2):
    return {'norm_w': norm_w[None, :], 'w_pw1': w_pw1.astype(BF16), 'b_pw1': b_pw1[None, :], 'w_dw': w_dw,
            'b_dw': b_dw[None, :], 'mid_norm_w': mid_norm_w[None, :], 'w_pw2': w_pw2.astype(BF16),
            'b_pw2': b_pw2[None, :]}


def _prep_ffn(norm_w, w_gate, w_up, w_down):
    return {'norm_w': norm_w[None, :], 'w_gate': w_gate.astype(BF16), 'w_up': w_up.astype(BF16),
            'w_down': w_down.astype(BF16)}


def _tiles(seq):
    tm = min(512, seq)
    return {'tm': tm, 'chunk': min(256, seq), 'tq': min(512, seq), 'tk': min(512, seq)}


def _run_group(x, hyb, conv, ffn0, ffn1, cfg):
    b, seq, _ = x.shape
    xf = x.reshape(b * seq, D_MODEL)
    tabs = _rope_tables(seq)
    qr, kr, vr, g, q, k, v = _hyb_in(xf, hyb, tabs, seq, cfg['tm'])
    r3 = lambda a: a.reshape(b, seq, a.shape[-1])
    ret = _retention(r3(qr), r3(kr), r3(vr), r3(g), hyb, cfg['chunk'])
    att = _attention(r3(q), r3(k), r3(v), cfg['tq'], cfg['tk'])
    x1 = _mix_ffn(xf, ret.reshape(b * seq, -1), att.reshape(b * seq, -1), hyb, ffn0, cfg['tm'])
    x2 = _conv_ffn(x1, conv, ffn1, seq, cfg['tm'])
    return x2.reshape(b, seq, D_MODEL)


def kernel(x_prompt, x_sample, hyb_norm_w, hyb_w_in, ret_decay_fwd, ret_decay_bwd, ret_gn_w, mla_w_q_b, mla_q_a_norm, mla_w_kv_b, mla_kv_a_norm, mla_q_norm, mla_k_norm, hyb_w_out, conv_norm_w, conv_w_pw1, conv_b_pw1, conv_w_dw, conv_b_dw, conv_mid_norm_w, conv_w_pw2, conv_b_pw2, ffn_norm_w, ffn_w_gate, ffn_w_up, ffn_w_down):
    conv = _prep_conv(conv_norm_w[0], conv_w_pw1[0], conv_b_pw1[0], conv_w_dw[0], conv_b_dw[0],
                      conv_mid_norm_w[0], conv_w_pw2[0], conv_b_pw2[0])
    ffn0 = _prep_ffn(ffn_norm_w[0], ffn_w_gate[0], ffn_w_up[0], ffn_w_down[0])
    ffn1 = _prep_ffn(ffn_norm_w[1], ffn_w_gate[1], ffn_w_up[1], ffn_w_down[1])

    def run(x):
        cfg = _tiles(x.shape[1])
        hyb = _prep_hybrid(hyb_norm_w[0], hyb_w_in[0], ret_decay_fwd[0], ret_decay_bwd[0], ret_gn_w[0],
                           mla_w_q_b[0], mla_q_a_norm[0], mla_w_kv_b[0], mla_kv_a_norm[0], mla_q_norm[0],
                           mla_k_norm[0], hyb_w_out[0], cfg['chunk'])
        return _run_group(x, hyb, conv, ffn0, ffn1, cfg)

    return (run(x_prompt), run(x_sample))
```

```python
import functools
import math

import jax
import jax.numpy as jnp
from jax import lax
from jax.experimental import pallas as pl
from jax.experimental.pallas import tpu as pltpu

D_MODEL = 1024
RET_HEADS = 4
RET_DIM = 128
RET_WIDTH = RET_HEADS * RET_DIM
MLA_HEADS = 4
MLA_Q_RANK = 256
MLA_KV_RANK = 128
MLA_NOPE = 128
MLA_ROPE = 64
MLA_V = 128
MLA_QK = MLA_NOPE + MLA_ROPE
MLA_QK_PAD = 256
ROPE_BASE = 10000.0
CONV_WIDTH = 31
CONV_PAD = (CONV_WIDTH - 1) // 2
HALO = 16
D_FF = 2816
NORM_EPS = 1e-6
LOG2E = math.log2(math.e)

LANE = 128
SUBLANE = 8
VMEM_LIMIT = 56 * 1024 * 1024

F32 = jnp.float32
BF16 = jnp.bfloat16


def _dot(a, b):
    return jnp.dot(a, b, preferred_element_type=F32)


def _dot_nt(a, b):
    return lax.dot_general(a, b, (((1,), (1,)), ((), ())), preferred_element_type=F32)


def _dot_tn(a, b):
    return lax.dot_general(a, b, (((0,), (0,)), ((), ())), preferred_element_type=F32)


def _rms(x, w):
    return x * lax.rsqrt(jnp.mean(x * x, axis=-1, keepdims=True) + NORM_EPS) * w


def _silu(g):
    return g / (1.0 + jnp.exp(-g))


def _const_spec(shape):
    return pl.BlockSpec(shape, lambda *_: (0,) * len(shape), pipeline_mode=pl.Buffered(1))


def _hyb_in_kernel(x_ref, nw_ref, win_ref, cr_ref, sr_ref, cm_ref, sm_ref, qan_ref, wqb_ref, kvan_ref,
                   wkvb_ref, qn_ref, kn_ref, qr_ref, kr_ref, vr_ref, g_ref, q_ref, k_ref, v_ref):
    h = _rms(x_ref[...], nw_ref[...]).astype(BF16)
    cr, sr = cr_ref[...], sr_ref[...]
    cm, sm = cm_ref[...], sm_ref[...]

    lat = _dot(h, win_ref[:, 4 * RET_WIDTH:4 * RET_WIDTH + 512])

    qk = _dot(h, win_ref[:, 0:2 * RET_WIDTH])
    for i in range(2 * RET_HEADS):
        seg = qk[:, i * LANE:(i + 1) * LANE]
        rot = seg * cr + pltpu.roll(seg, RET_DIM // 2, 1) * sr
        if i < RET_HEADS:
            qr_ref[:, i * LANE:(i + 1) * LANE] = (rot * (RET_DIM ** -0.5)).astype(BF16)
        else:
            j = i - RET_HEADS
            kr_ref[:, j * LANE:(j + 1) * LANE] = rot.astype(BF16)

    cq = _rms(lat[:, :MLA_Q_RANK], qan_ref[...]).astype(BF16)
    q4 = _dot(cq, wqb_ref[...])
    ckv = _rms(lat[:, MLA_Q_RANK:MLA_Q_RANK + MLA_KV_RANK], kvan_ref[...]).astype(BF16)
    kv4 = _dot(ckv, wkvb_ref[...])

    def rope64(t):
        return t * cm + pltpu.roll(t, MLA_ROPE // 2, 1) * sm

    krot = rope64(lat[:, MLA_Q_RANK + MLA_KV_RANK:])
    krot_sq = krot * krot
    qn, kn = qn_ref[...], kn_ref[...]
    for hh in range(MLA_HEADS):
        base = hh * MLA_QK_PAD
        nope = q4[:, base:base + LANE]
        rot = rope64(q4[:, base + LANE:base + 2 * LANE])
        ss = jnp.sum(nope * nope + rot * rot, axis=-1, keepdims=True)
        r = lax.rsqrt(ss * (1.0 / MLA_QK) + NORM_EPS) * (MLA_QK ** -0.5 * LOG2E)
        q_ref[:, base:base + LANE] = (nope * r * qn[:, :LANE]).astype(BF16)
        q_ref[:, base + LANE:base + 2 * LANE] = (rot * r * qn[:, LANE:]).astype(BF16)

        knope = kv4[:, base:base + LANE]
        rk = lax.rsqrt(jnp.sum(knope * knope + krot_sq, axis=-1, keepdims=True) * (1.0 / MLA_QK) + NORM_EPS)
        k_ref[:, base:base + LANE] = (knope * rk * kn[:, :LANE]).astype(BF16)
        k_ref[:, base + LANE:base + 2 * LANE] = (krot * rk * kn[:, LANE:]).astype(BF16)
        v_ref[:, hh * LANE:(hh + 1) * LANE] = kv4[:, base + LANE:base + 2 * LANE].astype(BF16)

    vg = _dot(h, win_ref[:, 2 * RET_WIDTH:4 * RET_WIDTH])
    vr_ref[...] = vg[:, :RET_WIDTH].astype(BF16)
    g_ref[...] = vg[:, RET_WIDTH:].astype(BF16)


def _hyb_in(x, p, tabs, seq, tm):
    t = x.shape[0]
    nps = seq // tm
    row = lambda w: pl.BlockSpec((tm, w), lambda i: (i, 0))
    tab = pl.BlockSpec((tm, LANE), lambda i: (i % nps, 0))
    out_w = (RET_WIDTH, RET_WIDTH, RET_WIDTH, RET_WIDTH, MLA_HEADS * MLA_QK_PAD, MLA_HEADS * MLA_QK_PAD,
             MLA_HEADS * MLA_V)
    return pl.pallas_call(
        _hyb_in_kernel,
        grid=(t // tm,),
        in_specs=[row(D_MODEL), _const_spec((1, D_MODEL)), _const_spec(p['w_in'].shape), tab, tab, tab, tab,
                  _const_spec((1, MLA_Q_RANK)), _const_spec(p['w_q_b'].shape), _const_spec((1, MLA_KV_RANK)),
                  _const_spec(p['w_kv_b'].shape), _const_spec((1, MLA_QK_PAD)), _const_spec((1, MLA_QK_PAD))],
        out_specs=[row(w) for w in out_w],
        out_shape=[jax.ShapeDtypeStruct((t, w), BF16) for w in out_w],
        compiler_params=pltpu.CompilerParams(dimension_semantics=("parallel",), vmem_limit_bytes=VMEM_LIMIT),
        name="hyb_in",
    )(x, p['hyb_norm_w'], p['w_in'], tabs['cr'], tabs['sr'], tabs['cm'], tabs['sm'], p['q_a_norm'], p['w_q_b'],
      p['kv_a_norm'], p['w_kv_b'], p['q_norm'], p['k_norm'])


RET_UNROLL = 8


def _ret_kernel(q_ref, k_ref, v_ref, g_ref, df_ref, db_ref, gn_ref, o_ref, dmat_ref, tab_ref, kv_ref, st_ref, *,
                chunk, seq):
    c = chunk
    n = seq // c
    lf = -jnp.exp(df_ref[...])
    lb = -jnp.exp(db_ref[...])
    row = lax.broadcasted_iota(jnp.int32, (c, c), 0).astype(F32)
    col = lax.broadcasted_iota(jnp.int32, (c, c), 1).astype(F32)
    diff = row - col
    dmat_ref[...] = jnp.where(diff >= 0, jnp.exp(lf * jnp.maximum(diff, 0.0)),
                              jnp.exp(lb * jnp.maximum(-diff, 0.0)))
    pos = lax.broadcasted_iota(jnp.int32, (c, LANE), 0).astype(F32)
    lf1, lb1 = lf[:, :LANE], lb[:, :LANE]
    tab_ref[0] = jnp.exp(lf1 * (pos + 1.0))
    tab_ref[1] = jnp.exp(lf1 * (c - 1.0 - pos))
    tab_ref[2] = jnp.exp(lb1 * (c - pos))
    tab_ref[3] = jnp.exp(lb1 * pos)
    gf = jnp.exp(lf1 * float(c))
    gb = jnp.exp(lb1 * float(c))

    def chunk_rows(i):
        return pl.ds(pl.multiple_of(i * c, c), c)

    def increments(i, carry):
        rows = chunk_rows(i)
        vc = v_ref[rows, :].astype(F32)
        v2 = jnp.concatenate([(vc * tab_ref[1]).astype(BF16), (vc * tab_ref[3]).astype(BF16)], axis=1)
        kv_ref[i] = _dot_tn(k_ref[rows, :], v2)
        return carry

    lax.fori_loop(0, n, increments, 0, unroll=RET_UNROLL)

    def scan_fwd(i, sf):
        st_ref[i, :RET_DIM, :] = sf.astype(BF16)
        return sf * gf + kv_ref[i, :, :RET_DIM]

    def scan_bwd(j, sb):
        i = n - 1 - j
        st_ref[i, RET_DIM:, :] = sb.astype(BF16)
        return sb * gb + kv_ref[i, :, RET_DIM:]

    zero_state = jnp.zeros((RET_DIM, RET_DIM), F32)
    lax.fori_loop(0, n, scan_fwd, zero_state)
    lax.fori_loop(0, n, scan_bwd, zero_state)

    def outputs(i, carry):
        rows = chunk_rows(i)
        qc = q_ref[rows, :]
        a = (_dot_nt(qc, k_ref[rows, :]) * dmat_ref[...]).astype(BF16)
        qf = qc.astype(F32)
        q2 = jnp.concatenate([(qf * tab_ref[0]).astype(BF16), (qf * tab_ref[2]).astype(BF16)], axis=1)
        o = _dot(a, v_ref[rows, :]) + _dot(q2, st_ref[i])
        y = _rms(o, gn_ref[...])
        o_ref[rows, :] = (y * _silu(g_ref[rows, :].astype(F32))).astype(BF16)
        return carry

    lax.fori_loop(0, n, outputs, 0, unroll=RET_UNROLL)


def _retention(qr, kr, vr, g, p, chunk):
    b, seq, _ = qr.shape
    n = seq // chunk
    head = pl.BlockSpec((None, seq, RET_DIM), lambda bi, hi: (bi, 0, hi))
    par = lambda w: pl.BlockSpec((None, 1, w), lambda bi, hi: (hi, 0, 0))
    return pl.pallas_call(
        functools.partial(_ret_kernel, chunk=chunk, seq=seq),
        grid=(b, RET_HEADS),
        in_specs=[head, head, head, head, par(chunk), par(chunk), par(RET_DIM)],
        out_specs=head,
        out_shape=jax.ShapeDtypeStruct((b, seq, RET_WIDTH), BF16),
        scratch_shapes=[pltpu.VMEM((chunk, chunk), F32),
                        pltpu.VMEM((4, chunk, LANE), F32),
                        pltpu.VMEM((n, RET_DIM, 2 * RET_DIM), F32),
                        pltpu.VMEM((n, 2 * RET_DIM, RET_DIM), BF16)],
        compiler_params=pltpu.CompilerParams(dimension_semantics=("parallel", "parallel"),
                                             vmem_limit_bytes=VMEM_LIMIT),
        name="retention",
    )(qr, kr, vr, g, p['decay_fwd'], p['decay_bwd'], p['gn_w'])


def _rows(start, size):
    if isinstance(start, int):
        return pl.ds(start, size)
    return pl.ds(pl.multiple_of(start, size), size)


def _attn_kernel(q_ref, k_ref, v_ref, o_ref, s0, s1, p0, p1, a0, a1, m_ref, acc_ref, v1_ref, *, tq, tk, seq, unroll):
    n = seq // tk
    total = (seq // tq) * n
    s_buf, p_buf, a_buf = (s0, s1), (p0, p1), (a0, a1)
    v1_ref[:, :MLA_V] = v_ref[...]
    v1_ref[:, MLA_V:] = jnp.ones((seq, MLA_V), BF16)
    m_ref[...] = jnp.full(m_ref.shape, -jnp.inf, F32)
    acc_ref[...] = jnp.zeros(acc_ref.shape, F32)

    def q_rows(t):
        return _rows((t // n) * tq, tq)

    def k_rows(t):
        return _rows((t % n) * tk, tk)

    def qk(t, slot):
        s_buf[slot][...] = _dot_nt(q_ref[q_rows(t), :], k_ref[k_rows(t), :])

    def softmax(t, slot):
        s = s_buf[slot][...]
        m_prev = jnp.where(t % n == 0, -jnp.inf, m_ref[...])
        m_new = jnp.maximum(m_prev, jnp.max(s, axis=1, keepdims=True))
        a_buf[slot][...] = jnp.exp2(m_prev - m_new)
        p_buf[slot][...] = jnp.exp2(s - jnp.concatenate([m_new] * (tk // LANE), axis=1)).astype(BF16)
        m_ref[...] = m_new

    def pv(t, slot):
        alpha = a_buf[slot][...]
        acc = acc_ref[...] * jnp.concatenate([alpha, alpha], axis=1) + _dot(p_buf[slot][...], v1_ref[k_rows(t), :])
        acc_ref[...] = acc
        o_ref[q_rows(t), :] = (acc[:, :MLA_V] / acc[:, MLA_V:]).astype(BF16)

    def step(t, slot, do_qk=True, do_softmax=True, do_pv=True):
        if do_qk:
            qk(t + 1, 1 - slot)
        if do_softmax:
            softmax(t, slot)
        if do_pv:
            pv(t - 1, 1 - slot)

    qk(0, 0)
    step(0, 0, do_pv=False)

    def body(i, carry):
        t0 = unroll * i + 1
        for u in range(unroll):
            step(t0 + u, (1 + u) % 2)
        return carry

    lax.fori_loop(0, (total - 2) // unroll, body, 0)
    step(total - 1, 1, do_qk=False)
    step(total, 0, do_qk=False, do_softmax=False)


def _attention(q, k, v, tq, tk, unroll):
    b, seq, _ = q.shape
    total = (seq // tq) * (seq // tk)
    assert unroll % 2 == 0 and (total - 2) % unroll == 0
    head = lambda w: pl.BlockSpec((None, seq, w), lambda bi, hi: (bi, 0, hi))
    return pl.pallas_call(
        functools.partial(_attn_kernel, tq=tq, tk=tk, seq=seq, unroll=unroll),
        grid=(b, MLA_HEADS),
        in_specs=[head(MLA_QK_PAD), head(MLA_QK_PAD), head(MLA_V)],
        out_specs=head(MLA_V),
        out_shape=jax.ShapeDtypeStruct((b, seq, MLA_HEADS * MLA_V), BF16),
        scratch_shapes=[pltpu.VMEM((tq, tk), F32), pltpu.VMEM((tq, tk), F32),
                        pltpu.VMEM((tq, tk), BF16), pltpu.VMEM((tq, tk), BF16),
                        pltpu.VMEM((tq, LANE), F32), pltpu.VMEM((tq, LANE), F32),
                        pltpu.VMEM((tq, LANE), F32),
                        pltpu.VMEM((tq, 2 * MLA_V), F32),
                        pltpu.VMEM((seq, 2 * MLA_V), BF16)],
        compiler_params=pltpu.CompilerParams(dimension_semantics=("parallel", "parallel"),
                                             vmem_limit_bytes=VMEM_LIMIT),
        name="attention",
    )(q, k, v)


FF_CHUNK = 256


def _ffn(x1, nw_ref, wg_ref, wu_ref, wd_ref, act_ref):
    h = _rms(x1, nw_ref[...]).astype(BF16)
    for c0 in range(0, D_FF, FF_CHUNK):
        g = _dot(h, wg_ref[:, c0:c0 + FF_CHUNK])
        u = _dot(h, wu_ref[:, c0:c0 + FF_CHUNK])
        act_ref[:, c0:c0 + FF_CHUNK] = (_silu(g) * u).astype(BF16)
    return x1 + _dot(act_ref[...], wd_ref[...])


def _ffn_specs():
    return [_const_spec((1, D_MODEL)), _const_spec((D_MODEL, D_FF)), _const_spec((D_MODEL, D_FF)),
            _const_spec((D_FF, D_MODEL))]


def _mix_ffn_kernel(x_ref, ret_ref, att_ref, wor_ref, woa_ref, nw_ref, wg_ref, wu_ref, wd_ref, o_ref, act_ref):
    x1 = x_ref[...] + _dot(ret_ref[...], wor_ref[...]) + _dot(att_ref[...], woa_ref[...])
    o_ref[...] = _ffn(x1, nw_ref, wg_ref, wu_ref, wd_ref, act_ref)


def _mix_ffn(x, ret, att, p, f, tm):
    t = x.shape[0]
    row = lambda w: pl.BlockSpec((tm, w), lambda i: (i, 0))
    return pl.pallas_call(
        _mix_ffn_kernel,
        grid=(t // tm,),
        in_specs=[row(D_MODEL), row(RET_WIDTH), row(MLA_HEADS * MLA_V), _const_spec((RET_WIDTH, D_MODEL)),
                  _const_spec((MLA_HEADS * MLA_V, D_MODEL))] + _ffn_specs(),
        out_specs=row(D_MODEL),
        out_shape=jax.ShapeDtypeStruct((t, D_MODEL), F32),
        scratch_shapes=[pltpu.VMEM((tm, D_FF), BF16)],
        compiler_params=pltpu.CompilerParams(dimension_semantics=("parallel",), vmem_limit_bytes=VMEM_LIMIT),
        name="mix_ffn",
    )(x, ret, att, p['w_out_ret'], p['w_out_att'], f['norm_w'], f['w_gate'], f['w_up'], f['w_down'])


CONV_ROWS = 64


def _conv_ffn_kernel(xp_ref, x_ref, xn_ref, nw_ref, w1_ref, b1_ref, wdw_ref, bdw_ref, mw_ref, w2_ref, b2_ref,
                     fnw_ref, wg_ref, wu_ref, wd_ref, o_ref, xcat_ref, hbuf_ref, cbuf_ref, act_ref, *, tm, nps):
    i = pl.program_id(0)
    first = (i % nps) == 0
    last = (i % nps) == nps - 1
    x = x_ref[...]
    nw = nw_ref[...]
    xcat_ref[0:HALO, :] = _rms(xp_ref[...], nw).astype(BF16)
    xcat_ref[HALO:HALO + tm, :] = _rms(x, nw).astype(BF16)
    xcat_ref[HALO + tm:, :] = _rms(xn_ref[...], nw).astype(BF16)
    hcat = xcat_ref[...]
    a = _dot(hcat, w1_ref[:, :D_MODEL]) + b1_ref[:, :D_MODEL]
    gate = _dot(hcat, w1_ref[:, D_MODEL:]) + b1_ref[:, D_MODEL:]
    hbuf_ref[...] = a / (1.0 + jnp.exp(-gate))

    @pl.when(first)
    def _():
        hbuf_ref[0:HALO, :] = jnp.zeros((HALO, D_MODEL), F32)

    @pl.when(last)
    def _():
        hbuf_ref[HALO + tm:, :] = jnp.zeros((HALO, D_MODEL), F32)

    def conv_block(r, carry):
        r0 = pl.multiple_of(r * CONV_ROWS, CONV_ROWS)
        for lb in range(D_MODEL // LANE):
            lanes = slice(lb * LANE, (lb + 1) * LANE)
            acc = None
            for sh in range(SUBLANE):
                part = None
                for kk in range(CONV_WIDTH):
                    off = HALO - CONV_PAD + kk
                    if off % SUBLANE != sh:
                        continue
                    al = off - sh
                    t = hbuf_ref[pl.ds(r0 + al, CONV_ROWS + SUBLANE), lanes] * wdw_ref[kk:kk + 1, lanes]
                    part = t if part is None else part + t
                part = part[sh:sh + CONV_ROWS, :]
                acc = part if acc is None else acc + part
            cbuf_ref[pl.ds(r0, CONV_ROWS), lanes] = acc
        return carry

    lax.fori_loop(0, tm // CONV_ROWS, conv_block, 0)

    h2 = _silu(_rms(cbuf_ref[...] + bdw_ref[...], mw_ref[...])).astype(BF16)
    x1 = x + _dot(h2, w2_ref[...]) + b2_ref[...]
    o_ref[...] = _ffn(x1, fnw_ref, wg_ref, wu_ref, wd_ref, act_ref)


def _conv_ffn(x, p, f, seq, tm):
    t = x.shape[0]
    nps = seq // tm
    hb = tm // HALO
    x3 = x.reshape(t // HALO, HALO, D_MODEL)
    nblk = t // HALO
    return pl.pallas_call(
        functools.partial(_conv_ffn_kernel, tm=tm, nps=nps),
        grid=(t // tm,),
        in_specs=[pl.BlockSpec((None, HALO, D_MODEL), lambda i: (jnp.maximum(i * hb - 1, 0), 0, 0)),
                  pl.BlockSpec((tm, D_MODEL), lambda i: (i, 0)),
                  pl.BlockSpec((None, HALO, D_MODEL), lambda i: (jnp.minimum((i + 1) * hb, nblk - 1), 0, 0)),
                  _const_spec((1, D_MODEL)), _const_spec((D_MODEL, 2 * D_MODEL)), _const_spec((1, 2 * D_MODEL)),
                  _const_spec((CONV_WIDTH, D_MODEL)), _const_spec((1, D_MODEL)), _const_spec((1, D_MODEL)),
                  _const_spec((D_MODEL, D_MODEL)), _const_spec((1, D_MODEL))] + _ffn_specs(),
        out_specs=pl.BlockSpec((tm, D_MODEL), lambda i: (i, 0)),
        out_shape=jax.ShapeDtypeStruct((t, D_MODEL), F32),
        scratch_shapes=[pltpu.VMEM((tm + 2 * HALO, D_MODEL), BF16), pltpu.VMEM((tm + 2 * HALO, D_MODEL), F32),
                        pltpu.VMEM((tm, D_MODEL), F32), pltpu.VMEM((tm, D_FF), BF16)],
        compiler_params=pltpu.CompilerParams(dimension_semantics=("parallel",), vmem_limit_bytes=VMEM_LIMIT),
        name="conv_ffn",
    )(x3, x, x3, p['norm_w'], p['w_pw1'], p['b_pw1'], p['w_dw'], p['b_dw'], p['mid_norm_w'], p['w_pw2'],
      p['b_pw2'], f['norm_w'], f['w_gate'], f['w_up'], f['w_down'])


def _rope_tables(seq):
    def cos_sin(half):
        inv = ROPE_BASE ** (-jnp.arange(half, dtype=F32) / half)
        ang = jnp.arange(seq, dtype=F32)[:, None] * inv[None, :]
        return jnp.cos(ang), jnp.sin(ang)

    c, s = cos_sin(RET_DIM // 2)
    cm, sm = cos_sin(MLA_ROPE // 2)
    z = jnp.zeros_like(cm)
    return {'cr': jnp.concatenate([c, c], axis=1), 'sr': jnp.concatenate([-s, s], axis=1),
            'cm': jnp.concatenate([cm, cm, z, z], axis=1), 'sm': jnp.concatenate([-sm, sm, z, z], axis=1)}


def _prep_hybrid(norm_w, w_in, decay_fwd, decay_bwd, gn_w, w_q_b, q_a_norm, w_kv_b, kv_a_norm, q_norm, k_norm,
                 w_out, chunk):
    kpe = w_in[:, -MLA_ROPE:]
    wq = w_q_b.reshape(MLA_Q_RANK, MLA_HEADS, MLA_QK)
    wq = jnp.concatenate([wq, wq[:, :, MLA_NOPE:]], axis=-1).reshape(MLA_Q_RANK, MLA_HEADS * MLA_QK_PAD)
    pad = lambda w: jnp.concatenate([w, jnp.zeros((MLA_QK_PAD - MLA_QK,), F32)])[None, :]
    bc = lambda d: jnp.broadcast_to(d[:, None, None], (RET_HEADS, 1, chunk))
    return {
        'hyb_norm_w': norm_w[None, :], 'w_in': jnp.concatenate([w_in, kpe], axis=1).astype(BF16),
        'decay_fwd': bc(decay_fwd), 'decay_bwd': bc(decay_bwd), 'gn_w': gn_w[:, None, :],
        'w_q_b': wq.astype(BF16), 'q_a_norm': q_a_norm[None, :], 'w_kv_b': w_kv_b.astype(BF16),
        'kv_a_norm': kv_a_norm[None, :], 'q_norm': pad(q_norm), 'k_norm': pad(k_norm),
        'w_out_ret': w_out[:RET_WIDTH].astype(BF16), 'w_out_att': w_out[RET_WIDTH:].astype(BF16),
    }


def _prep_conv(norm_w, w_pw1, b_pw1, w_dw, b_dw, mid_norm_w, w_pw2, b_pw2):
    return {'norm_w': norm_w[None, :], 'w_pw1': w_pw1.astype(BF16), 'b_pw1': b_pw1[None, :], 'w_dw': w_dw,
            'b_dw': b_dw[None, :], 'mid_norm_w': mid_norm_w[None, :], 'w_pw2': w_pw2.astype(BF16),
            'b_pw2': b_pw2[None, :]}


def _prep_ffn(norm_w, w_gate, w_up, w_down):
    return {'norm_w': norm_w[None, :], 'w_gate': w_gate.astype(BF16), 'w_up': w_up.astype(BF16),
            'w_down': w_down.astype(BF16)}


def _tiles(seq):
    tm = min(512, seq)
    return {'tm': tm, 'chunk': min(256, seq), 'tq': min(512, seq), 'tk': min(1024, seq), 'attn_unroll': 6}


def _run_group(x, hyb, conv, ffn0, ffn1, cfg):
    b, seq, _ = x.shape
    xf = x.reshape(b * seq, D_MODEL)
    tabs = _rope_tables(seq)
    qr, kr, vr, g, q, k, v = _hyb_in(xf, hyb, tabs, seq, cfg['tm'])
    r3 = lambda a: a.reshape(b, seq, a.shape[-1])
    ret = _retention(r3(qr), r3(kr), r3(vr), r3(g), hyb, cfg['chunk'])
    att = _attention(r3(q), r3(k), r3(v), cfg['tq'], cfg['tk'], cfg['attn_unroll'])
    x1 = _mix_ffn(xf, ret.reshape(b * seq, -1), att.reshape(b * seq, -1), hyb, ffn0, cfg['tm'])
    x2 = _conv_ffn(x1, conv, ffn1, seq, cfg['tm'])
    return x2.reshape(b, seq, D_MODEL)


def kernel(x_prompt, x_sample, hyb_norm_w, hyb_w_in, ret_decay_fwd, ret_decay_bwd, ret_gn_w, mla_w_q_b, mla_q_a_norm, mla_w_kv_b, mla_kv_a_norm, mla_q_norm, mla_k_norm, hyb_w_out, conv_norm_w, conv_w_pw1, conv_b_pw1, conv_w_dw, conv_b_dw, conv_mid_norm_w, conv_w_pw2, conv_b_pw2, ffn_norm_w, ffn_w_gate, ffn_w_up, ffn_w_down):
    conv = _prep_conv(conv_norm_w[0], conv_w_pw1[0], conv_b_pw1[0], conv_w_dw[0], conv_b_dw[0],
                      conv_mid_norm_w[0], conv_w_pw2[0], conv_b_pw2[0])
    ffn0 = _prep_ffn(ffn_norm_w[0], ffn_w_gate[0], ffn_w_up[0], ffn_w_down[0])
    ffn1 = _prep_ffn(ffn_norm_w[1], ffn_w_gate[1], ffn_w_up[1], ffn_w_down[1])

    def run(x):
        cfg = _tiles(x.shape[1])
        hyb = _prep_hybrid(hyb_norm_w[0], hyb_w_in[0], ret_decay_fwd[0], ret_decay_bwd[0], ret_gn_w[0],
                           mla_w_q_b[0], mla_q_a_norm[0], mla_w_kv_b[0], mla_kv_a_norm[0], mla_q_norm[0],
                           mla_k_norm[0], hyb_w_out[0], cfg['chunk'])
        return _run_group(x, hyb, conv, ffn0, ffn1, cfg)

    return (run(x_prompt), run(x_sample))
```

```python
import functools
import math

import jax
import jax.numpy as jnp
from jax import lax
from jax.experimental import pallas as pl
from jax.experimental.pallas import tpu as pltpu

D_MODEL = 1024
RET_HEADS = 4
RET_DIM = 128
RET_WIDTH = RET_HEADS * RET_DIM
MLA_HEADS = 4
MLA_Q_RANK = 256
MLA_KV_RANK = 128
MLA_NOPE = 128
MLA_ROPE = 64
MLA_V = 128
MLA_QK = MLA_NOPE + MLA_ROPE
MLA_QK_PAD = 256
ROPE_BASE = 10000.0
CONV_WIDTH = 31
CONV_PAD = (CONV_WIDTH - 1) // 2
HALO = 16
D_FF = 2816
NORM_EPS = 1e-6
LOG2E = math.log2(math.e)

LANE = 128
SUBLANE = 8
VMEM_LIMIT = 56 * 1024 * 1024

F32 = jnp.float32
BF16 = jnp.bfloat16


def _dot(a, b):
    return jnp.dot(a, b, preferred_element_type=F32)


def _dot_nt(a, b):
    return lax.dot_general(a, b, (((1,), (1,)), ((), ())), preferred_element_type=F32)


def _dot_tn(a, b):
    return lax.dot_general(a, b, (((0,), (0,)), ((), ())), preferred_element_type=F32)


def _rms(x, w):
    return x * lax.rsqrt(jnp.mean(x * x, axis=-1, keepdims=True) + NORM_EPS) * w


def _silu(g):
    return g / (1.0 + jnp.exp(-g))


def _const_spec(shape):
    return pl.BlockSpec(shape, lambda *_: (0,) * len(shape), pipeline_mode=pl.Buffered(1))


def _hyb_in_kernel(x_ref, nw_ref, win_ref, cr_ref, sr_ref, cm_ref, sm_ref, qan_ref, wqb_ref, kvan_ref,
                   wkvb_ref, qn_ref, kn_ref, qr_ref, kr_ref, vr_ref, g_ref, q_ref, k_ref, v_ref):
    h = _rms(x_ref[...], nw_ref[...]).astype(BF16)
    cr, sr = cr_ref[...], sr_ref[...]
    cm, sm = cm_ref[...], sm_ref[...]

    lat = _dot(h, win_ref[:, 4 * RET_WIDTH:4 * RET_WIDTH + 512])

    qk = _dot(h, win_ref[:, 0:2 * RET_WIDTH])
    for i in range(2 * RET_HEADS):
        seg = qk[:, i * LANE:(i + 1) * LANE]
        rot = seg * cr + pltpu.roll(seg, RET_DIM // 2, 1) * sr
        if i < RET_HEADS:
            qr_ref[:, i * LANE:(i + 1) * LANE] = (rot * (RET_DIM ** -0.5)).astype(BF16)
        else:
            j = i - RET_HEADS
            kr_ref[:, j * LANE:(j + 1) * LANE] = rot.astype(BF16)

    cq = _rms(lat[:, :MLA_Q_RANK], qan_ref[...]).astype(BF16)
    q4 = _dot(cq, wqb_ref[...])
    ckv = _rms(lat[:, MLA_Q_RANK:MLA_Q_RANK + MLA_KV_RANK], kvan_ref[...]).astype(BF16)
    kv4 = _dot(ckv, wkvb_ref[...])

    def rope64(t):
        return t * cm + pltpu.roll(t, MLA_ROPE // 2, 1) * sm

    krot = rope64(lat[:, MLA_Q_RANK + MLA_KV_RANK:])
    krot_sq = krot * krot
    qn, kn = qn_ref[...], kn_ref[...]
    for hh in range(MLA_HEADS):
        base = hh * MLA_QK_PAD
        nope = q4[:, base:base + LANE]
        rot = rope64(q4[:, base + LANE:base + 2 * LANE])
        ss = jnp.sum(nope * nope + rot * rot, axis=-1, keepdims=True)
        r = lax.rsqrt(ss * (1.0 / MLA_QK) + NORM_EPS) * (MLA_QK ** -0.5 * LOG2E)
        q_ref[:, base:base + LANE] = (nope * r * qn[:, :LANE]).astype(BF16)
        q_ref[:, base + LANE:base + 2 * LANE] = (rot * r * qn[:, LANE:]).astype(BF16)

        knope = kv4[:, base:base + LANE]
        rk = lax.rsqrt(jnp.sum(knope * knope + krot_sq, axis=-1, keepdims=True) * (1.0 / MLA_QK) + NORM_EPS)
        k_ref[:, base:base + LANE] = (knope * rk * kn[:, :LANE]).astype(BF16)
        k_ref[:, base + LANE:base + 2 * LANE] = (krot * rk * kn[:, LANE:]).astype(BF16)
        v_ref[:, hh * LANE:(hh + 1) * LANE] = kv4[:, base + LANE:base + 2 * LANE].astype(BF16)

    vg = _dot(h, win_ref[:, 2 * RET_WIDTH:4 * RET_WIDTH])
    vr_ref[...] = vg[:, :RET_WIDTH].astype(BF16)
    g_ref[...] = vg[:, RET_WIDTH:].astype(BF16)


def _hyb_in(x, p, tabs, seq, tm):
    t = x.shape[0]
    nps = seq // tm
    row = lambda w: pl.BlockSpec((tm, w), lambda i: (i, 0))
    tab = pl.BlockSpec((tm, LANE), lambda i: (i % nps, 0))
    out_w = (RET_WIDTH, RET_WIDTH, RET_WIDTH, RET_WIDTH, MLA_HEADS * MLA_QK_PAD, MLA_HEADS * MLA_QK_PAD,
             MLA_HEADS * MLA_V)
    return pl.pallas_call(
        _hyb_in_kernel,
        grid=(t // tm,),
        in_specs=[row(D_MODEL), _const_spec((1, D_MODEL)), _const_spec(p['w_in'].shape), tab, tab, tab, tab,
                  _const_spec((1, MLA_Q_RANK)), _const_spec(p['w_q_b'].shape), _const_spec((1, MLA_KV_RANK)),
                  _const_spec(p['w_kv_b'].shape), _const_spec((1, MLA_QK_PAD)), _const_spec((1, MLA_QK_PAD))],
        out_specs=[row(w) for w in out_w],
        out_shape=[jax.ShapeDtypeStruct((t, w), BF16) for w in out_w],
        compiler_params=pltpu.CompilerParams(dimension_semantics=("parallel",), vmem_limit_bytes=VMEM_LIMIT),
        name="hyb_in",
    )(x, p['hyb_norm_w'], p['w_in'], tabs['cr'], tabs['sr'], tabs['cm'], tabs['sm'], p['q_a_norm'], p['w_q_b'],
      p['kv_a_norm'], p['w_kv_b'], p['q_norm'], p['k_norm'])


RET_UNROLL = 8


def _ret_kernel(q_ref, k_ref, v_ref, g_ref, df_ref, db_ref, gn_ref, o_ref, dmat_ref, tab_ref, kv_ref, st_ref, *,
                chunk, seq):
    c = chunk
    n = seq // c
    lf = -jnp.exp(df_ref[...])
    lb = -jnp.exp(db_ref[...])
    row = lax.broadcasted_iota(jnp.int32, (c, c), 0).astype(F32)
    col = lax.broadcasted_iota(jnp.int32, (c, c), 1).astype(F32)
    diff = row - col
    dmat_ref[...] = jnp.where(diff >= 0, jnp.exp(lf * jnp.maximum(diff, 0.0)),
                              jnp.exp(lb * jnp.maximum(-diff, 0.0)))
    pos = lax.broadcasted_iota(jnp.int32, (c, LANE), 0).astype(F32)
    lf1, lb1 = lf[:, :LANE], lb[:, :LANE]
    tab_ref[0] = jnp.exp(lf1 * (pos + 1.0))
    tab_ref[1] = jnp.exp(lf1 * (c - 1.0 - pos))
    tab_ref[2] = jnp.exp(lb1 * (c - pos))
    tab_ref[3] = jnp.exp(lb1 * pos)
    gf = jnp.exp(lf1 * float(c))
    gb = jnp.exp(lb1 * float(c))

    def chunk_rows(i):
        return pl.ds(pl.multiple_of(i * c, c), c)

    def increments(i, carry):
        rows = chunk_rows(i)
        vc = v_ref[rows, :].astype(F32)
        v2 = jnp.concatenate([(vc * tab_ref[1]).astype(BF16), (vc * tab_ref[3]).astype(BF16)], axis=1)
        kv_ref[i] = _dot_tn(k_ref[rows, :], v2)
        return carry

    lax.fori_loop(0, n, increments, 0, unroll=RET_UNROLL)

    def scan_fwd(i, sf):
        st_ref[i, :RET_DIM, :] = sf.astype(BF16)
        return sf * gf + kv_ref[i, :, :RET_DIM]

    def scan_bwd(j, sb):
        i = n - 1 - j
        st_ref[i, RET_DIM:, :] = sb.astype(BF16)
        return sb * gb + kv_ref[i, :, RET_DIM:]

    zero_state = jnp.zeros((RET_DIM, RET_DIM), F32)
    lax.fori_loop(0, n, scan_fwd, zero_state)
    lax.fori_loop(0, n, scan_bwd, zero_state)

    def outputs(i, carry):
        rows = chunk_rows(i)
        qc = q_ref[rows, :]
        a = (_dot_nt(qc, k_ref[rows, :]) * dmat_ref[...]).astype(BF16)
        qf = qc.astype(F32)
        q2 = jnp.concatenate([(qf * tab_ref[0]).astype(BF16), (qf * tab_ref[2]).astype(BF16)], axis=1)
        o = _dot(a, v_ref[rows, :]) + _dot(q2, st_ref[i])
        y = _rms(o, gn_ref[...])
        o_ref[rows, :] = (y * _silu(g_ref[rows, :].astype(F32))).astype(BF16)
        return carry

    lax.fori_loop(0, n, outputs, 0, unroll=RET_UNROLL)


def _retention(qr, kr, vr, g, p, chunk):
    b, seq, _ = qr.shape
    n = seq // chunk
    head = pl.BlockSpec((None, seq, RET_DIM), lambda bi, hi: (bi, 0, hi))
    par = lambda w: pl.BlockSpec((None, 1, w), lambda bi, hi: (hi, 0, 0))
    return pl.pallas_call(
        functools.partial(_ret_kernel, chunk=chunk, seq=seq),
        grid=(b, RET_HEADS),
        in_specs=[head, head, head, head, par(chunk), par(chunk), par(RET_DIM)],
        out_specs=head,
        out_shape=jax.ShapeDtypeStruct((b, seq, RET_WIDTH), BF16),
        scratch_shapes=[pltpu.VMEM((chunk, chunk), F32),
                        pltpu.VMEM((4, chunk, LANE), F32),
                        pltpu.VMEM((n, RET_DIM, 2 * RET_DIM), F32),
                        pltpu.VMEM((n, 2 * RET_DIM, RET_DIM), BF16)],
        compiler_params=pltpu.CompilerParams(dimension_semantics=("parallel", "parallel"),
                                             vmem_limit_bytes=VMEM_LIMIT),
        name="retention",
    )(qr, kr, vr, g, p['decay_fwd'], p['decay_bwd'], p['gn_w'])


def _rows(start, size):
    if isinstance(start, int):
        return pl.ds(start, size)
    return pl.ds(pl.multiple_of(start, size), size)


def _attn_kernel(q_ref, k_ref, v_ref, o_ref, s0, s1, p0, p1, a0, a1, m_ref, acc_ref, v1_ref, *, tq, tk, seq, unroll):
    n = seq // tk
    total = (seq // tq) * n
    s_buf, p_buf, a_buf = (s0, s1), (p0, p1), (a0, a1)
    v1_ref[:, :MLA_V] = v_ref[...]
    v1_ref[:, MLA_V:] = jnp.ones((seq, MLA_V), BF16)
    m_ref[...] = jnp.full(m_ref.shape, -jnp.inf, F32)
    acc_ref[...] = jnp.zeros(acc_ref.shape, F32)

    def q_rows(t):
        return _rows((t // n) * tq, tq)

    def k_rows(t):
        return _rows((t % n) * tk, tk)

    def qk(t, slot):
        s_buf[slot][...] = _dot_nt(q_ref[q_rows(t), :], k_ref[k_rows(t), :])

    def softmax(t, slot):
        s = s_buf[slot][...]
        m_prev = jnp.where(t % n == 0, -jnp.inf, m_ref[...])
        m_new = jnp.maximum(m_prev, jnp.max(s, axis=1, keepdims=True))
        a_buf[slot][...] = jnp.exp2(m_prev - m_new)
        p_buf[slot][...] = jnp.exp2(s - jnp.concatenate([m_new] * (tk // LANE), axis=1)).astype(BF16)
        m_ref[...] = m_new

    def pv(t, slot):
        alpha = a_buf[slot][...]
        acc = acc_ref[...] * jnp.concatenate([alpha, alpha], axis=1) + _dot(p_buf[slot][...], v1_ref[k_rows(t), :])
        acc_ref[...] = acc
        o_ref[q_rows(t), :] = (acc[:, :MLA_V] / acc[:, MLA_V:]).astype(BF16)

    def step(t, slot, do_qk=True, do_softmax=True, do_pv=True):
        if do_qk:
            qk(t + 1, 1 - slot)
        if do_softmax:
            softmax(t, slot)
        if do_pv:
            pv(t - 1, 1 - slot)

    qk(0, 0)
    step(0, 0, do_pv=False)

    def body(i, carry):
        t0 = unroll * i + 1
        for u in range(unroll):
            step(t0 + u, (1 + u) % 2)
        return carry

    lax.fori_loop(0, (total - 2) // unroll, body, 0)
    step(total - 1, 1, do_qk=False)
    step(total, 0, do_qk=False, do_softmax=False)


def _attention(q, k, v, tq, tk, unroll):
    b, seq, _ = q.shape
    total = (seq // tq) * (seq // tk)
    assert unroll % 2 == 0 and (total - 2) % unroll == 0
    head = lambda w: pl.BlockSpec((None, seq, w), lambda bi, hi: (bi, 0, hi))
    return pl.pallas_call(
        functools.partial(_attn_kernel, tq=tq, tk=tk, seq=seq, unroll=unroll),
        grid=(b, MLA_HEADS),
        in_specs=[head(MLA_QK_PAD), head(MLA_QK_PAD), head(MLA_V)],
        out_specs=head(MLA_V),
        out_shape=jax.ShapeDtypeStruct((b, seq, MLA_HEADS * MLA_V), BF16),
        scratch_shapes=[pltpu.VMEM((tq, tk), F32), pltpu.VMEM((tq, tk), F32),
                        pltpu.VMEM((tq, tk), BF16), pltpu.VMEM((tq, tk), BF16),
                        pltpu.VMEM((tq, LANE), F32), pltpu.VMEM((tq, LANE), F32),
                        pltpu.VMEM((tq, LANE), F32),
                        pltpu.VMEM((tq, 2 * MLA_V), F32),
                        pltpu.VMEM((seq, 2 * MLA_V), BF16)],
        compiler_params=pltpu.CompilerParams(dimension_semantics=("parallel", "parallel"),
                                             vmem_limit_bytes=VMEM_LIMIT),
        name="attention",
    )(q, k, v)


FF_CHUNK = 256


def _ffn(x1, nw_ref, wg_ref, wu_ref, wd_ref, act_ref):
    h = _rms(x1, nw_ref[...]).astype(BF16)
    for c0 in range(0, D_FF, FF_CHUNK):
        g = _dot(h, wg_ref[:, c0:c0 + FF_CHUNK])
        u = _dot(h, wu_ref[:, c0:c0 + FF_CHUNK])
        act_ref[:, c0:c0 + FF_CHUNK] = (_silu(g) * u).astype(BF16)
    return x1 + _dot(act_ref[...], wd_ref[...])


def _ffn_specs():
    return [_const_spec((1, D_MODEL)), _const_spec((D_MODEL, D_FF)), _const_spec((D_MODEL, D_FF)),
            _const_spec((D_FF, D_MODEL))]


def _mix_ffn_kernel(x_ref, ret_ref, att_ref, wor_ref, woa_ref, nw_ref, wg_ref, wu_ref, wd_ref, o_ref, act_ref):
    x1 = x_ref[...] + _dot(ret_ref[...], wor_ref[...]) + _dot(att_ref[...], woa_ref[...])
    o_ref[...] = _ffn(x1, nw_ref, wg_ref, wu_ref, wd_ref, act_ref)


def _mix_ffn(x, ret, att, p, f, tm):
    t = x.shape[0]
    row = lambda w: pl.BlockSpec((tm, w), lambda i: (i, 0))
    return pl.pallas_call(
        _mix_ffn_kernel,
        grid=(t // tm,),
        in_specs=[row(D_MODEL), row(RET_WIDTH), row(MLA_HEADS * MLA_V), _const_spec((RET_WIDTH, D_MODEL)),
                  _const_spec((MLA_HEADS * MLA_V, D_MODEL))] + _ffn_specs(),
        out_specs=row(D_MODEL),
        out_shape=jax.ShapeDtypeStruct((t, D_MODEL), F32),
        scratch_shapes=[pltpu.VMEM((tm, D_FF), BF16)],
        compiler_params=pltpu.CompilerParams(dimension_semantics=("parallel",), vmem_limit_bytes=VMEM_LIMIT),
        name="mix_ffn",
    )(x, ret, att, p['w_out_ret'], p['w_out_att'], f['norm_w'], f['w_gate'], f['w_up'], f['w_down'])


CONV_ROWS = 32
N_FF_CHUNKS = D_FF // FF_CHUNK


def _conv_unit(hbuf_ref, wdw_ref, cbuf_ref, r0, l0):
    lanes = pl.ds(pl.multiple_of(l0, LANE), LANE)
    acc = None
    for sh in range(SUBLANE):
        part = None
        for kk in range(CONV_WIDTH):
            off = HALO - CONV_PAD + kk
            if off % SUBLANE != sh:
                continue
            rows = pl.ds(pl.multiple_of(r0 + (off - sh), SUBLANE), CONV_ROWS + SUBLANE)
            t = hbuf_ref[rows, lanes] * wdw_ref[kk:kk + 1, lanes]
            part = t if part is None else part + t
        part = part[sh:sh + CONV_ROWS, :]
        acc = part if acc is None else acc + part
    cbuf_ref[pl.ds(pl.multiple_of(r0, CONV_ROWS), CONV_ROWS), lanes] = acc


def _conv_ffn_kernel(xp_ref, x_ref, xn_ref, xprev_ref, nw_ref, w1_ref, b1_ref, wdw_ref, bdw_ref, mw_ref, w2_ref,
                     b2_ref, fnw_ref, wg_ref, wu_ref, wd_ref, o_ref, xcat_ref, hbuf_ref, cbuf_ref, hn_ref, x1_ref,
                     act_ref, *, tm, nps, nt):
    i = pl.program_id(0)
    ti = jnp.minimum(i, nt - 1)
    first = (ti % nps) == 0
    last = (ti % nps) == nps - 1

    @pl.when(i == 0)
    def _():
        cbuf_ref[...] = jnp.zeros(cbuf_ref.shape, F32)

    h2 = _silu(_rms(cbuf_ref[...] + bdw_ref[...], mw_ref[...])).astype(BF16)
    x1 = xprev_ref[...] + _dot(h2, w2_ref[...]) + b2_ref[...]
    x1_ref[...] = x1
    hn_ref[...] = _rms(x1, fnw_ref[...]).astype(BF16)

    nw = nw_ref[...]
    xcat_ref[0:HALO, :] = _rms(xp_ref[...], nw).astype(BF16)
    xcat_ref[HALO:HALO + tm, :] = _rms(x_ref[...], nw).astype(BF16)
    xcat_ref[HALO + tm:, :] = _rms(xn_ref[...], nw).astype(BF16)
    hcat = xcat_ref[...]
    a = _dot(hcat, w1_ref[:, :D_MODEL]) + b1_ref[:, :D_MODEL]
    gate = _dot(hcat, w1_ref[:, D_MODEL:]) + b1_ref[:, D_MODEL:]
    hbuf_ref[...] = a / (1.0 + jnp.exp(-gate))

    @pl.when(first)
    def _():
        hbuf_ref[0:HALO, :] = jnp.zeros((HALO, D_MODEL), F32)

    @pl.when(last)
    def _():
        hbuf_ref[HALO + tm:, :] = jnp.zeros((HALO, D_MODEL), F32)

    lane_blocks = D_MODEL // LANE
    n_units = (tm // CONV_ROWS) * lane_blocks
    per_chunk = -(-n_units // N_FF_CHUNKS)

    def chunk(c, carry):
        cols = pl.ds(pl.multiple_of(c * FF_CHUNK, FF_CHUNK), FF_CHUNK)
        g = _dot(hn_ref[...], wg_ref[:, cols])
        u = _dot(hn_ref[...], wu_ref[:, cols])
        act_ref[:, cols] = (_silu(g) * u).astype(BF16)
        for j in range(per_chunk):
            unit = jnp.minimum(c * per_chunk + j, n_units - 1)
            _conv_unit(hbuf_ref, wdw_ref, cbuf_ref, (unit // lane_blocks) * CONV_ROWS, (unit % lane_blocks) * LANE)
        return carry

    lax.fori_loop(0, N_FF_CHUNKS, chunk, 0)
    o_ref[...] = x1_ref[...] + _dot(act_ref[...], wd_ref[...])


def _conv_ffn(x, p, f, seq, tm):
    t = x.shape[0]
    nps, nt = seq // tm, t // tm
    hb = tm // HALO
    x3 = x.reshape(t // HALO, HALO, D_MODEL)
    nblk = t // HALO
    return pl.pallas_call(
        functools.partial(_conv_ffn_kernel, tm=tm, nps=nps, nt=nt),
        grid=(nt + 1,),
        in_specs=[pl.BlockSpec((None, HALO, D_MODEL), lambda i: (jnp.clip(i * hb - 1, 0, nblk - 1), 0, 0)),
                  pl.BlockSpec((tm, D_MODEL), lambda i: (jnp.minimum(i, nt - 1), 0)),
                  pl.BlockSpec((None, HALO, D_MODEL), lambda i: (jnp.minimum((i + 1) * hb, nblk - 1), 0, 0)),
                  pl.BlockSpec((tm, D_MODEL), lambda i: (jnp.maximum(i - 1, 0), 0)),
                  _const_spec((1, D_MODEL)), _const_spec((D_MODEL, 2 * D_MODEL)), _const_spec((1, 2 * D_MODEL)),
                  _const_spec((CONV_WIDTH, D_MODEL)), _const_spec((1, D_MODEL)), _const_spec((1, D_MODEL)),
                  _const_spec((D_MODEL, D_MODEL)), _const_spec((1, D_MODEL))] + _ffn_specs(),
        out_specs=pl.BlockSpec((tm, D_MODEL), lambda i: (jnp.maximum(i - 1, 0), 0)),
        out_shape=jax.ShapeDtypeStruct((t, D_MODEL), F32),
        scratch_shapes=[pltpu.VMEM((tm + 2 * HALO, D_MODEL), BF16),
                        pltpu.VMEM((tm + 2 * HALO, D_MODEL), F32),
                        pltpu.VMEM((tm, D_MODEL), F32),
                        pltpu.VMEM((tm, D_MODEL), BF16),
                        pltpu.VMEM((tm, D_MODEL), F32),
                        pltpu.VMEM((tm, D_FF), BF16)],
        compiler_params=pltpu.CompilerParams(dimension_semantics=("arbitrary",), vmem_limit_bytes=VMEM_LIMIT),
        name="conv_ffn",
    )(x3, x, x3, x, p['norm_w'], p['w_pw1'], p['b_pw1'], p['w_dw'], p['b_dw'], p['mid_norm_w'], p['w_pw2'],
      p['b_pw2'], f['norm_w'], f['w_gate'], f['w_up'], f['w_down'])


def _rope_tables(seq):
    def cos_sin(half):
        inv = ROPE_BASE ** (-jnp.arange(half, dtype=F32) / half)
        ang = jnp.arange(seq, dtype=F32)[:, None] * inv[None, :]
        return jnp.cos(ang), jnp.sin(ang)

    c, s = cos_sin(RET_DIM // 2)
    cm, sm = cos_sin(MLA_ROPE // 2)
    z = jnp.zeros_like(cm)
    return {'cr': jnp.concatenate([c, c], axis=1), 'sr': jnp.concatenate([-s, s], axis=1),
            'cm': jnp.concatenate([cm, cm, z, z], axis=1), 'sm': jnp.concatenate([-sm, sm, z, z], axis=1)}


def _prep_hybrid(norm_w, w_in, decay_fwd, decay_bwd, gn_w, w_q_b, q_a_norm, w_kv_b, kv_a_norm, q_norm, k_norm,
                 w_out, chunk):
    kpe = w_in[:, -MLA_ROPE:]
    wq = w_q_b.reshape(MLA_Q_RANK, MLA_HEADS, MLA_QK)
    wq = jnp.concatenate([wq, wq[:, :, MLA_NOPE:]], axis=-1).reshape(MLA_Q_RANK, MLA_HEADS * MLA_QK_PAD)
    pad = lambda w: jnp.concatenate([w, jnp.zeros((MLA_QK_PAD - MLA_QK,), F32)])[None, :]
    bc = lambda d: jnp.broadcast_to(d[:, None, None], (RET_HEADS, 1, chunk))
    return {
        'hyb_norm_w': norm_w[None, :], 'w_in': jnp.concatenate([w_in, kpe], axis=1).astype(BF16),
        'decay_fwd': bc(decay_fwd), 'decay_bwd': bc(decay_bwd), 'gn_w': gn_w[:, None, :],
        'w_q_b': wq.astype(BF16), 'q_a_norm': q_a_norm[None, :], 'w_kv_b': w_kv_b.astype(BF16),
        'kv_a_norm': kv_a_norm[None, :], 'q_norm': pad(q_norm), 'k_norm': pad(k_norm),
        'w_out_ret': w_out[:RET_WIDTH].astype(BF16), 'w_out_att': w_out[RET_WIDTH:].astype(BF16),
    }


def _prep_conv(norm_w, w_pw1, b_pw1, w_dw, b_dw, mid_norm_w, w_pw2, b_pw2):
    return {'norm_w': norm_w[None, :], 'w_pw1': w_pw1.astype(BF16), 'b_pw1': b_pw1[None, :], 'w_dw': w_dw,
            'b_dw': b_dw[None, :], 'mid_norm_w': mid_norm_w[None, :], 'w_pw2': w_pw2.astype(BF16),
            'b_pw2': b_pw2[None, :]}


def _prep_ffn(norm_w, w_gate, w_up, w_down):
    return {'norm_w': norm_w[None, :], 'w_gate': w_gate.astype(BF16), 'w_up': w_up.astype(BF16),
            'w_down': w_down.astype(BF16)}


def _tiles(seq):
    tm = min(512, seq)
    return {'tm': tm, 'chunk': min(256, seq), 'tq': min(512, seq), 'tk': min(1024, seq), 'attn_unroll': 6}


def _run_group(x, hyb, conv, ffn0, ffn1, cfg):
    b, seq, _ = x.shape
    xf = x.reshape(b * seq, D_MODEL)
    tabs = _rope_tables(seq)
    qr, kr, vr, g, q, k, v = _hyb_in(xf, hyb, tabs, seq, cfg['tm'])
    r3 = lambda a: a.reshape(b, seq, a.shape[-1])
    ret = _retention(r3(qr), r3(kr), r3(vr), r3(g), hyb, cfg['chunk'])
    att = _attention(r3(q), r3(k), r3(v), cfg['tq'], cfg['tk'], cfg['attn_unroll'])
    x1 = _mix_ffn(xf, ret.reshape(b * seq, -1), att.reshape(b * seq, -1), hyb, ffn0, cfg['tm'])
    x2 = _conv_ffn(x1, conv, ffn1, seq, cfg['tm'])
    return x2.reshape(b, seq, D_MODEL)


def kernel(x_prompt, x_sample, hyb_norm_w, hyb_w_in, ret_decay_fwd, ret_decay_bwd, ret_gn_w, mla_w_q_b, mla_q_a_norm, mla_w_kv_b, mla_kv_a_norm, mla_q_norm, mla_k_norm, hyb_w_out, conv_norm_w, conv_w_pw1, conv_b_pw1, conv_w_dw, conv_b_dw, conv_mid_norm_w, conv_w_pw2, conv_b_pw2, ffn_norm_w, ffn_w_gate, ffn_w_up, ffn_w_down):
    conv = _prep_conv(conv_norm_w[0], conv_w_pw1[0], conv_b_pw1[0], conv_w_dw[0], conv_b_dw[0],
                      conv_mid_norm_w[0], conv_w_pw2[0], conv_b_pw2[0])
    ffn0 = _prep_ffn(ffn_norm_w[0], ffn_w_gate[0], ffn_w_up[0], ffn_w_down[0])
    ffn1 = _prep_ffn(ffn_norm_w[1], ffn_w_gate[1], ffn_w_up[1], ffn_w_down[1])

    def run(x):
        cfg = _tiles(x.shape[1])
        hyb = _prep_hybrid(hyb_norm_w[0], hyb_w_in[0], ret_decay_fwd[0], ret_decay_bwd[0], ret_gn_w[0],
                           mla_w_q_b[0], mla_q_a_norm[0], mla_w_kv_b[0], mla_kv_a_norm[0], mla_q_norm[0],
                           mla_k_norm[0], hyb_w_out[0], cfg['chunk'])
        return _run_group(x, hyb, conv, ffn0, ffn1, cfg)

    return (run(x_prompt), run(x_sample))
```

```python
import functools
import math

import jax
import jax.numpy as jnp
from jax import lax
from jax.experimental import pallas as pl
from jax.experimental.pallas import tpu as pltpu

D_MODEL = 1024
RET_HEADS = 4
RET_DIM = 128
RET_WIDTH = RET_HEADS * RET_DIM
MLA_HEADS = 4
MLA_Q_RANK = 256
MLA_KV_RANK = 128
MLA_NOPE = 128
MLA_ROPE = 64
MLA_V = 128
MLA_QK = MLA_NOPE + MLA_ROPE
MLA_QK_PAD = 256
ROPE_BASE = 10000.0
CONV_WIDTH = 31
CONV_PAD = (CONV_WIDTH - 1) // 2
HALO = 16
D_FF = 2816
NORM_EPS = 1e-6
LOG2E = math.log2(math.e)

LANE = 128
SUBLANE = 8
VMEM_LIMIT = 56 * 1024 * 1024

F32 = jnp.float32
BF16 = jnp.bfloat16


def _dot(a, b):
    return jnp.dot(a, b, preferred_element_type=F32)


def _dot_nt(a, b):
    return lax.dot_general(a, b, (((1,), (1,)), ((), ())), preferred_element_type=F32)


def _dot_tn(a, b):
    return lax.dot_general(a, b, (((0,), (0,)), ((), ())), preferred_element_type=F32)


def _rms(x, w):
    return x * lax.rsqrt(jnp.mean(x * x, axis=-1, keepdims=True) + NORM_EPS) * w


def _silu(g):
    return g / (1.0 + jnp.exp(-g))


def _const_spec(shape):
    return pl.BlockSpec(shape, lambda *_: (0,) * len(shape), pipeline_mode=pl.Buffered(1))


def _hyb_in_kernel(x_ref, nw_ref, win_ref, cr_ref, sr_ref, cm_ref, sm_ref, qan_ref, wqb_ref, kvan_ref,
                   wkvb_ref, qn_ref, kn_ref, qr_ref, kr_ref, vr_ref, g_ref, q_ref, k_ref, v_ref):
    h = _rms(x_ref[...], nw_ref[...]).astype(BF16)
    cr, sr = cr_ref[...], sr_ref[...]
    cm, sm = cm_ref[...], sm_ref[...]

    lat = _dot(h, win_ref[:, 4 * RET_WIDTH:4 * RET_WIDTH + 512])

    qk = _dot(h, win_ref[:, 0:2 * RET_WIDTH])
    for i in range(2 * RET_HEADS):
        seg = qk[:, i * LANE:(i + 1) * LANE]
        rot = seg * cr + pltpu.roll(seg, RET_DIM // 2, 1) * sr
        if i < RET_HEADS:
            qr_ref[:, i * LANE:(i + 1) * LANE] = (rot * (RET_DIM ** -0.5)).astype(BF16)
        else:
            j = i - RET_HEADS
            kr_ref[:, j * LANE:(j + 1) * LANE] = rot.astype(BF16)

    cq = _rms(lat[:, :MLA_Q_RANK], qan_ref[...]).astype(BF16)
    q4 = _dot(cq, wqb_ref[...])
    ckv = _rms(lat[:, MLA_Q_RANK:MLA_Q_RANK + MLA_KV_RANK], kvan_ref[...]).astype(BF16)
    kv4 = _dot(ckv, wkvb_ref[...])

    def rope64(t):
        return t * cm + pltpu.roll(t, MLA_ROPE // 2, 1) * sm

    krot = rope64(lat[:, MLA_Q_RANK + MLA_KV_RANK:])
    krot_sq = krot * krot
    qn, kn = qn_ref[...], kn_ref[...]
    for hh in range(MLA_HEADS):
        base = hh * MLA_QK_PAD
        nope = q4[:, base:base + LANE]
        rot = rope64(q4[:, base + LANE:base + 2 * LANE])
        ss = jnp.sum(nope * nope + rot * rot, axis=-1, keepdims=True)
        r = lax.rsqrt(ss * (1.0 / MLA_QK) + NORM_EPS) * (MLA_QK ** -0.5 * LOG2E)
        q_ref[:, base:base + LANE] = (nope * r * qn[:, :LANE]).astype(BF16)
        q_ref[:, base + LANE:base + 2 * LANE] = (rot * r * qn[:, LANE:]).astype(BF16)

        knope = kv4[:, base:base + LANE]
        rk = lax.rsqrt(jnp.sum(knope * knope + krot_sq, axis=-1, keepdims=True) * (1.0 / MLA_QK) + NORM_EPS)
        k_ref[:, base:base + LANE] = (knope * rk * kn[:, :LANE]).astype(BF16)
        k_ref[:, base + LANE:base + 2 * LANE] = (krot * rk * kn[:, LANE:]).astype(BF16)
        v_ref[:, hh * LANE:(hh + 1) * LANE] = kv4[:, base + LANE:base + 2 * LANE].astype(BF16)

    vg = _dot(h, win_ref[:, 2 * RET_WIDTH:4 * RET_WIDTH])
    vr_ref[...] = vg[:, :RET_WIDTH].astype(BF16)
    g_ref[...] = vg[:, RET_WIDTH:].astype(BF16)


def _hyb_in(x, p, tabs, seq, tm):
    t = x.shape[0]
    nps = seq // tm
    row = lambda w: pl.BlockSpec((tm, w), lambda i: (i, 0))
    tab = pl.BlockSpec((tm, LANE), lambda i: (i % nps, 0))
    out_w = (RET_WIDTH, RET_WIDTH, RET_WIDTH, RET_WIDTH, MLA_HEADS * MLA_QK_PAD, MLA_HEADS * MLA_QK_PAD,
             MLA_HEADS * MLA_V)
    return pl.pallas_call(
        _hyb_in_kernel,
        grid=(t // tm,),
        in_specs=[row(D_MODEL), _const_spec((1, D_MODEL)), _const_spec(p['w_in'].shape), tab, tab, tab, tab,
                  _const_spec((1, MLA_Q_RANK)), _const_spec(p['w_q_b'].shape), _const_spec((1, MLA_KV_RANK)),
                  _const_spec(p['w_kv_b'].shape), _const_spec((1, MLA_QK_PAD)), _const_spec((1, MLA_QK_PAD))],
        out_specs=[row(w) for w in out_w],
        out_shape=[jax.ShapeDtypeStruct((t, w), BF16) for w in out_w],
        compiler_params=pltpu.CompilerParams(dimension_semantics=("parallel",), vmem_limit_bytes=VMEM_LIMIT),
        name="hyb_in",
    )(x, p['hyb_norm_w'], p['w_in'], tabs['cr'], tabs['sr'], tabs['cm'], tabs['sm'], p['q_a_norm'], p['w_q_b'],
      p['kv_a_norm'], p['w_kv_b'], p['q_norm'], p['k_norm'])


RET_UNROLL = 8


def _ret_kernel(q_ref, k_ref, v_ref, g_ref, df_ref, db_ref, gn_ref, o_ref, dmat_ref, tab_ref, kv_ref, st_ref, *,
                chunk, seq):
    c = chunk
    n = seq // c
    lf = -jnp.exp(df_ref[...])
    lb = -jnp.exp(db_ref[...])
    row = lax.broadcasted_iota(jnp.int32, (c, c), 0).astype(F32)
    col = lax.broadcasted_iota(jnp.int32, (c, c), 1).astype(F32)
    diff = row - col
    dmat_ref[...] = jnp.where(diff >= 0, jnp.exp(lf * jnp.maximum(diff, 0.0)),
                              jnp.exp(lb * jnp.maximum(-diff, 0.0)))
    pos = lax.broadcasted_iota(jnp.int32, (c, LANE), 0).astype(F32)
    lf1, lb1 = lf[:, :LANE], lb[:, :LANE]
    tab_ref[0] = jnp.exp(lf1 * (pos + 1.0))
    tab_ref[1] = jnp.exp(lf1 * (c - 1.0 - pos))
    tab_ref[2] = jnp.exp(lb1 * (c - pos))
    tab_ref[3] = jnp.exp(lb1 * pos)
    gf = jnp.exp(lf1 * float(c))
    gb = jnp.exp(lb1 * float(c))

    def chunk_rows(i):
        return pl.ds(pl.multiple_of(i * c, c), c)

    def increments(i, carry):
        rows = chunk_rows(i)
        vc = v_ref[rows, :].astype(F32)
        v2 = jnp.concatenate([(vc * tab_ref[1]).astype(BF16), (vc * tab_ref[3]).astype(BF16)], axis=1)
        kv_ref[i] = _dot_tn(k_ref[rows, :], v2)
        return carry

    lax.fori_loop(0, n, increments, 0, unroll=RET_UNROLL)

    def scan_fwd(i, sf):
        st_ref[i, :RET_DIM, :] = sf.astype(BF16)
        return sf * gf + kv_ref[i, :, :RET_DIM]

    def scan_bwd(j, sb):
        i = n - 1 - j
        st_ref[i, RET_DIM:, :] = sb.astype(BF16)
        return sb * gb + kv_ref[i, :, RET_DIM:]

    zero_state = jnp.zeros((RET_DIM, RET_DIM), F32)
    lax.fori_loop(0, n, scan_fwd, zero_state)
    lax.fori_loop(0, n, scan_bwd, zero_state)

    def outputs(i, carry):
        rows = chunk_rows(i)
        qc = q_ref[rows, :]
        a = (_dot_nt(qc, k_ref[rows, :]) * dmat_ref[...]).astype(BF16)
        qf = qc.astype(F32)
        q2 = jnp.concatenate([(qf * tab_ref[0]).astype(BF16), (qf * tab_ref[2]).astype(BF16)], axis=1)
        o = _dot(a, v_ref[rows, :]) + _dot(q2, st_ref[i])
        y = _rms(o, gn_ref[...])
        o_ref[rows, :] = (y * _silu(g_ref[rows, :].astype(F32))).astype(BF16)
        return carry

    lax.fori_loop(0, n, outputs, 0, unroll=RET_UNROLL)


def _retention(qr, kr, vr, g, p, chunk):
    b, seq, _ = qr.shape
    n = seq // chunk
    head = pl.BlockSpec((None, seq, RET_DIM), lambda bi, hi: (bi, 0, hi))
    par = lambda w: pl.BlockSpec((None, 1, w), lambda bi, hi: (hi, 0, 0))
    return pl.pallas_call(
        functools.partial(_ret_kernel, chunk=chunk, seq=seq),
        grid=(b, RET_HEADS),
        in_specs=[head, head, head, head, par(chunk), par(chunk), par(RET_DIM)],
        out_specs=head,
        out_shape=jax.ShapeDtypeStruct((b, seq, RET_WIDTH), BF16),
        scratch_shapes=[pltpu.VMEM((chunk, chunk), F32),
                        pltpu.VMEM((4, chunk, LANE), F32),
                        pltpu.VMEM((n, RET_DIM, 2 * RET_DIM), F32),
                        pltpu.VMEM((n, 2 * RET_DIM, RET_DIM), BF16)],
        compiler_params=pltpu.CompilerParams(dimension_semantics=("parallel", "parallel"),
                                             vmem_limit_bytes=VMEM_LIMIT),
        name="retention",
    )(qr, kr, vr, g, p['decay_fwd'], p['decay_bwd'], p['gn_w'])


def _rows(start, size):
    if isinstance(start, int):
        return pl.ds(start, size)
    return pl.ds(pl.multiple_of(start, size), size)


def _attn_kernel(q_ref, k_ref, v_ref, o_ref, s0, s1, p0, p1, a0, a1, m_ref, acc_ref, v1_ref, *, tq, tk, seq, unroll):
    n = seq // tk
    total = (seq // tq) * n
    s_buf, p_buf, a_buf = (s0, s1), (p0, p1), (a0, a1)
    v1_ref[:, :MLA_V] = v_ref[...]
    v1_ref[:, MLA_V:] = jnp.ones((seq, MLA_V), BF16)
    m_ref[...] = jnp.full(m_ref.shape, -jnp.inf, F32)
    acc_ref[...] = jnp.zeros(acc_ref.shape, F32)

    def q_rows(t):
        return _rows((t // n) * tq, tq)

    def k_rows(t):
        return _rows((t % n) * tk, tk)

    def qk(t, slot):
        s_buf[slot][...] = _dot_nt(q_ref[q_rows(t), :], k_ref[k_rows(t), :])

    def softmax(t, slot):
        s = s_buf[slot][...]
        m_prev = jnp.where(t % n == 0, -jnp.inf, m_ref[...])
        m_new = jnp.maximum(m_prev, jnp.max(s, axis=1, keepdims=True))
        a_buf[slot][...] = jnp.exp2(m_prev - m_new)
        p_buf[slot][...] = jnp.exp2(s - jnp.concatenate([m_new] * (tk // LANE), axis=1)).astype(BF16)
        m_ref[...] = m_new

    def pv(t, slot):
        alpha = a_buf[slot][...]
        acc = acc_ref[...] * jnp.concatenate([alpha, alpha], axis=1) + _dot(p_buf[slot][...], v1_ref[k_rows(t), :])
        acc_ref[...] = acc
        o_ref[q_rows(t), :] = (acc[:, :MLA_V] / acc[:, MLA_V:]).astype(BF16)

    def step(t, slot, do_qk=True, do_softmax=True, do_pv=True):
        if do_qk:
            qk(t + 1, 1 - slot)
        if do_softmax:
            softmax(t, slot)
        if do_pv:
            pv(t - 1, 1 - slot)

    qk(0, 0)
    step(0, 0, do_pv=False)

    def body(i, carry):
        t0 = unroll * i + 1
        for u in range(unroll):
            step(t0 + u, (1 + u) % 2)
        return carry

    lax.fori_loop(0, (total - 2) // unroll, body, 0)
    step(total - 1, 1, do_qk=False)
    step(total, 0, do_qk=False, do_softmax=False)


def _attention(q, k, v, tq, tk, unroll):
    b, seq, _ = q.shape
    total = (seq // tq) * (seq // tk)
    assert unroll % 2 == 0 and (total - 2) % unroll == 0
    head = lambda w: pl.BlockSpec((None, seq, w), lambda bi, hi: (bi, 0, hi))
    return pl.pallas_call(
        functools.partial(_attn_kernel, tq=tq, tk=tk, seq=seq, unroll=unroll),
        grid=(b, MLA_HEADS),
        in_specs=[head(MLA_QK_PAD), head(MLA_QK_PAD), head(MLA_V)],
        out_specs=head(MLA_V),
        out_shape=jax.ShapeDtypeStruct((b, seq, MLA_HEADS * MLA_V), BF16),
        scratch_shapes=[pltpu.VMEM((tq, tk), F32), pltpu.VMEM((tq, tk), F32),
                        pltpu.VMEM((tq, tk), BF16), pltpu.VMEM((tq, tk), BF16),
                        pltpu.VMEM((tq, LANE), F32), pltpu.VMEM((tq, LANE), F32),
                        pltpu.VMEM((tq, LANE), F32),
                        pltpu.VMEM((tq, 2 * MLA_V), F32),
                        pltpu.VMEM((seq, 2 * MLA_V), BF16)],
        compiler_params=pltpu.CompilerParams(dimension_semantics=("parallel", "parallel"),
                                             vmem_limit_bytes=VMEM_LIMIT),
        name="attention",
    )(q, k, v)


FF_CHUNK = 256


def _ffn(x1, nw_ref, wg_ref, wu_ref, wd_ref, act_ref):
    h = _rms(x1, nw_ref[...]).astype(BF16)
    for c0 in range(0, D_FF, FF_CHUNK):
        g = _dot(h, wg_ref[:, c0:c0 + FF_CHUNK])
        u = _dot(h, wu_ref[:, c0:c0 + FF_CHUNK])
        act_ref[:, c0:c0 + FF_CHUNK] = (_silu(g) * u).astype(BF16)
    return x1 + _dot(act_ref[...], wd_ref[...])


def _ffn_specs():
    return [_const_spec((1, D_MODEL)), _const_spec((D_MODEL, D_FF)), _const_spec((D_MODEL, D_FF)),
            _const_spec((D_FF, D_MODEL))]


def _mix_ffn_kernel(x_ref, ret_ref, att_ref, wor_ref, woa_ref, nw_ref, wg_ref, wu_ref, wd_ref, o_ref, act_ref):
    x1 = x_ref[...] + _dot(ret_ref[...], wor_ref[...]) + _dot(att_ref[...], woa_ref[...])
    o_ref[...] = _ffn(x1, nw_ref, wg_ref, wu_ref, wd_ref, act_ref)


def _mix_ffn(x, ret, att, p, f, tm):
    t = x.shape[0]
    row = lambda w: pl.BlockSpec((tm, w), lambda i: (i, 0))
    return pl.pallas_call(
        _mix_ffn_kernel,
        grid=(t // tm,),
        in_specs=[row(D_MODEL), row(RET_WIDTH), row(MLA_HEADS * MLA_V), _const_spec((RET_WIDTH, D_MODEL)),
                  _const_spec((MLA_HEADS * MLA_V, D_MODEL))] + _ffn_specs(),
        out_specs=row(D_MODEL),
        out_shape=jax.ShapeDtypeStruct((t, D_MODEL), F32),
        scratch_shapes=[pltpu.VMEM((tm, D_FF), BF16)],
        compiler_params=pltpu.CompilerParams(dimension_semantics=("parallel",), vmem_limit_bytes=VMEM_LIMIT),
        name="mix_ffn",
    )(x, ret, att, p['w_out_ret'], p['w_out_att'], f['norm_w'], f['w_gate'], f['w_up'], f['w_down'])


CONV_ROWS = 32
N_FF_CHUNKS = D_FF // FF_CHUNK


def _conv_unit(hbuf_ref, wdw_ref, cbuf_ref, r0, l0):
    lanes = slice(l0, l0 + LANE)
    acc = None
    for sh in range(SUBLANE):
        part = None
        for kk in range(CONV_WIDTH):
            off = HALO - CONV_PAD + kk
            if off % SUBLANE != sh:
                continue
            al = r0 + off - sh
            t = hbuf_ref[al:al + CONV_ROWS + SUBLANE, lanes] * wdw_ref[kk:kk + 1, lanes]
            part = t if part is None else part + t
        part = part[sh:sh + CONV_ROWS, :]
        acc = part if acc is None else acc + part
    cbuf_ref[r0:r0 + CONV_ROWS, lanes] = acc


def _conv_ffn_kernel(xp_ref, x_ref, xn_ref, xprev_ref, nw_ref, w1_ref, b1_ref, wdw_ref, bdw_ref, mw_ref, w2_ref,
                     b2_ref, fnw_ref, wg_ref, wu_ref, wd_ref, o_ref, xcat_ref, hbuf_ref, cbuf_ref, hn_ref, x1_ref,
                     act_ref, *, tm, nps, nt):
    i = pl.program_id(0)
    ti = jnp.minimum(i, nt - 1)
    first = (ti % nps) == 0
    last = (ti % nps) == nps - 1

    @pl.when(i == 0)
    def _():
        cbuf_ref[...] = jnp.zeros(cbuf_ref.shape, F32)

    h2 = _silu(_rms(cbuf_ref[...] + bdw_ref[...], mw_ref[...])).astype(BF16)
    x1 = xprev_ref[...] + _dot(h2, w2_ref[...]) + b2_ref[...]
    x1_ref[...] = x1
    hn_ref[...] = _rms(x1, fnw_ref[...]).astype(BF16)

    nw = nw_ref[...]
    xcat_ref[0:HALO, :] = _rms(xp_ref[...], nw).astype(BF16)
    xcat_ref[HALO:HALO + tm, :] = _rms(x_ref[...], nw).astype(BF16)
    xcat_ref[HALO + tm:, :] = _rms(xn_ref[...], nw).astype(BF16)
    hcat = xcat_ref[...]
    a = _dot(hcat, w1_ref[:, :D_MODEL]) + b1_ref[:, :D_MODEL]
    gate = _dot(hcat, w1_ref[:, D_MODEL:]) + b1_ref[:, D_MODEL:]
    hbuf_ref[...] = a / (1.0 + jnp.exp(-gate))

    @pl.when(first)
    def _():
        hbuf_ref[0:HALO, :] = jnp.zeros((HALO, D_MODEL), F32)

    @pl.when(last)
    def _():
        hbuf_ref[HALO + tm:, :] = jnp.zeros((HALO, D_MODEL), F32)

    lane_blocks = D_MODEL // LANE
    units = [((u // lane_blocks) * CONV_ROWS, (u % lane_blocks) * LANE) for u in range((tm // CONV_ROWS) * lane_blocks)]
    per_dot = -(-len(units) // (2 * N_FF_CHUNKS))

    def conv_units(k):
        for r0, l0 in units[k * per_dot:(k + 1) * per_dot]:
            _conv_unit(hbuf_ref, wdw_ref, cbuf_ref, r0, l0)

    hn = hn_ref[...]
    for c in range(N_FF_CHUNKS):
        cols = slice(c * FF_CHUNK, (c + 1) * FF_CHUNK)
        g = _dot(hn, wg_ref[:, cols])
        conv_units(2 * c)
        u = _dot(hn, wu_ref[:, cols])
        act_ref[:, cols] = (_silu(g) * u).astype(BF16)
        conv_units(2 * c + 1)
    o_ref[...] = x1_ref[...] + _dot(act_ref[...], wd_ref[...])


def _conv_ffn(x, p, f, seq, tm):
    t = x.shape[0]
    nps, nt = seq // tm, t // tm
    hb = tm // HALO
    x3 = x.reshape(t // HALO, HALO, D_MODEL)
    nblk = t // HALO
    return pl.pallas_call(
        functools.partial(_conv_ffn_kernel, tm=tm, nps=nps, nt=nt),
        grid=(nt + 1,),
        in_specs=[pl.BlockSpec((None, HALO, D_MODEL), lambda i: (jnp.clip(i * hb - 1, 0, nblk - 1), 0, 0)),
                  pl.BlockSpec((tm, D_MODEL), lambda i: (jnp.minimum(i, nt - 1), 0)),
                  pl.BlockSpec((None, HALO, D_MODEL), lambda i: (jnp.minimum((i + 1) * hb, nblk - 1), 0, 0)),
                  pl.BlockSpec((tm, D_MODEL), lambda i: (jnp.maximum(i - 1, 0), 0)),
                  _const_spec((1, D_MODEL)), _const_spec((D_MODEL, 2 * D_MODEL)), _const_spec((1, 2 * D_MODEL)),
                  _const_spec((CONV_WIDTH, D_MODEL)), _const_spec((1, D_MODEL)), _const_spec((1, D_MODEL)),
                  _const_spec((D_MODEL, D_MODEL)), _const_spec((1, D_MODEL))] + _ffn_specs(),
        out_specs=pl.BlockSpec((tm, D_MODEL), lambda i: (jnp.maximum(i - 1, 0), 0)),
        out_shape=jax.ShapeDtypeStruct((t, D_MODEL), F32),
        scratch_shapes=[pltpu.VMEM((tm + 2 * HALO, D_MODEL), BF16),
                        pltpu.VMEM((tm + 2 * HALO, D_MODEL), F32),
                        pltpu.VMEM((tm, D_MODEL), F32),
                        pltpu.VMEM((tm, D_MODEL), BF16),
                        pltpu.VMEM((tm, D_MODEL), F32),
                        pltpu.VMEM((tm, D_FF), BF16)],
        compiler_params=pltpu.CompilerParams(dimension_semantics=("arbitrary",), vmem_limit_bytes=VMEM_LIMIT),
        name="conv_ffn",
    )(x3, x, x3, x, p['norm_w'], p['w_pw1'], p['b_pw1'], p['w_dw'], p['b_dw'], p['mid_norm_w'], p['w_pw2'],
      p['b_pw2'], f['norm_w'], f['w_gate'], f['w_up'], f['w_down'])


def _rope_tables(seq):
    def cos_sin(half):
        inv = ROPE_BASE ** (-jnp.arange(half, dtype=F32) / half)
        ang = jnp.arange(seq, dtype=F32)[:, None] * inv[None, :]
        return jnp.cos(ang), jnp.sin(ang)

    c, s = cos_sin(RET_DIM // 2)
    cm, sm = cos_sin(MLA_ROPE // 2)
    z = jnp.zeros_like(cm)
    return {'cr': jnp.concatenate([c, c], axis=1), 'sr': jnp.concatenate([-s, s], axis=1),
            'cm': jnp.concatenate([cm, cm, z, z], axis=1), 'sm': jnp.concatenate([-sm, sm, z, z], axis=1)}


def _prep_hybrid(norm_w, w_in, decay_fwd, decay_bwd, gn_w, w_q_b, q_a_norm, w_kv_b, kv_a_norm, q_norm, k_norm,
                 w_out, chunk):
    kpe = w_in[:, -MLA_ROPE:]
    wq = w_q_b.reshape(MLA_Q_RANK, MLA_HEADS, MLA_QK)
    wq = jnp.concatenate([wq, wq[:, :, MLA_NOPE:]], axis=-1).reshape(MLA_Q_RANK, MLA_HEADS * MLA_QK_PAD)
    pad = lambda w: jnp.concatenate([w, jnp.zeros((MLA_QK_PAD - MLA_QK,), F32)])[None, :]
    bc = lambda d: jnp.broadcast_to(d[:, None, None], (RET_HEADS, 1, chunk))
    return {
        'hyb_norm_w': norm_w[None, :], 'w_in': jnp.concatenate([w_in, kpe], axis=1).astype(BF16),
        'decay_fwd': bc(decay_fwd), 'decay_bwd': bc(decay_bwd), 'gn_w': gn_w[:, None, :],
        'w_q_b': wq.astype(BF16), 'q_a_norm': q_a_norm[None, :], 'w_kv_b': w_kv_b.astype(BF16),
        'kv_a_norm': kv_a_norm[None, :], 'q_norm': pad(q_norm), 'k_norm': pad(k_norm),
        'w_out_ret': w_out[:RET_WIDTH].astype(BF16), 'w_out_att': w_out[RET_WIDTH:].astype(BF16),
    }


def _prep_conv(norm_w, w_pw1, b_pw1, w_dw, b_dw, mid_norm_w, w_pw2, b_pw2):
    return {'norm_w': norm_w[None, :], 'w_pw1': w_pw1.astype(BF16), 'b_pw1': b_pw1[None, :], 'w_dw': w_dw,
            'b_dw': b_dw[None, :], 'mid_norm_w': mid_norm_w[None, :], 'w_pw2': w_pw2.astype(BF16),
            'b_pw2': b_pw2[None, :]}


def _prep_ffn(norm_w, w_gate, w_up, w_down):
    return {'norm_w': norm_w[None, :], 'w_gate': w_gate.astype(BF16), 'w_up': w_up.astype(BF16),
            'w_down': w_down.astype(BF16)}


def _tiles(seq):
    tm = min(512, seq)
    tq, tk = (256, 2048) if seq == 4096 else (min(512, seq), min(1024, seq))
    return {'tm': tm, 'chunk': min(256, seq), 'tq': tq, 'tk': tk, 'attn_unroll': 6}


def _run_group(x, hyb, conv, ffn0, ffn1, cfg):
    b, seq, _ = x.shape
    xf = x.reshape(b * seq, D_MODEL)
    tabs = _rope_tables(seq)
    qr, kr, vr, g, q, k, v = _hyb_in(xf, hyb, tabs, seq, cfg['tm'])
    r3 = lambda a: a.reshape(b, seq, a.shape[-1])
    ret = _retention(r3(qr), r3(kr), r3(vr), r3(g), hyb, cfg['chunk'])
    att = _attention(r3(q), r3(k), r3(v), cfg['tq'], cfg['tk'], cfg['attn_unroll'])
    x1 = _mix_ffn(xf, ret.reshape(b * seq, -1), att.reshape(b * seq, -1), hyb, ffn0, cfg['tm'])
    x2 = _conv_ffn(x1, conv, ffn1, seq, cfg['tm'])
    return x2.reshape(b, seq, D_MODEL)


def kernel(x_prompt, x_sample, hyb_norm_w, hyb_w_in, ret_decay_fwd, ret_decay_bwd, ret_gn_w, mla_w_q_b, mla_q_a_norm, mla_w_kv_b, mla_kv_a_norm, mla_q_norm, mla_k_norm, hyb_w_out, conv_norm_w, conv_w_pw1, conv_b_pw1, conv_w_dw, conv_b_dw, conv_mid_norm_w, conv_w_pw2, conv_b_pw2, ffn_norm_w, ffn_w_gate, ffn_w_up, ffn_w_down):
    conv = _prep_conv(conv_norm_w[0], conv_w_pw1[0], conv_b_pw1[0], conv_w_dw[0], conv_b_dw[0],
                      conv_mid_norm_w[0], conv_w_pw2[0], conv_b_pw2[0])
    ffn0 = _prep_ffn(ffn_norm_w[0], ffn_w_gate[0], ffn_w_up[0], ffn_w_down[0])
    ffn1 = _prep_ffn(ffn_norm_w[1], ffn_w_gate[1], ffn_w_up[1], ffn_w_down[1])

    def run(x):
        cfg = _tiles(x.shape[1])
        hyb = _prep_hybrid(hyb_norm_w[0], hyb_w_in[0], ret_decay_fwd[0], ret_decay_bwd[0], ret_gn_w[0],
                           mla_w_q_b[0], mla_q_a_norm[0], mla_w_kv_b[0], mla_kv_a_norm[0], mla_q_norm[0],
                           mla_k_norm[0], hyb_w_out[0], cfg['chunk'])
        return _run_group(x, hyb, conv, ffn0, ffn1, cfg)

    return (run(x_prompt), run(x_sample))
```

```python
import functools
import math

import jax
import jax.numpy as jnp
from jax import lax
from jax.experimental import pallas as pl
from jax.experimental.pallas import tpu as pltpu

D_MODEL = 1024
RET_HEADS = 4
RET_DIM = 128
RET_WIDTH = RET_HEADS * RET_DIM
MLA_HEADS = 4
MLA_Q_RANK = 256
MLA_KV_RANK = 128
MLA_NOPE = 128
MLA_ROPE = 64
MLA_V = 128
MLA_QK = MLA_NOPE + MLA_ROPE
MLA_QK_PAD = 256
ROPE_BASE = 10000.0
CONV_WIDTH = 31
CONV_PAD = (CONV_WIDTH - 1) // 2
HALO = 16
D_FF = 2816
NORM_EPS = 1e-6
LOG2E = math.log2(math.e)

LANE = 128
SUBLANE = 8
VMEM_LIMIT = 56 * 1024 * 1024

F32 = jnp.float32
BF16 = jnp.bfloat16


def _dot(a, b):
    return jnp.dot(a, b, preferred_element_type=F32)


def _dot_nt(a, b):
    return lax.dot_general(a, b, (((1,), (1,)), ((), ())), preferred_element_type=F32)


def _dot_tn(a, b):
    return lax.dot_general(a, b, (((0,), (0,)), ((), ())), preferred_element_type=F32)


def _rms(x, w):
    return x * lax.rsqrt(jnp.mean(x * x, axis=-1, keepdims=True) + NORM_EPS) * w


def _silu(g):
    return g / (1.0 + jnp.exp(-g))


def _const_spec(shape):
    return pl.BlockSpec(shape, lambda *_: (0,) * len(shape), pipeline_mode=pl.Buffered(1))


def _hyb_in_kernel(x_ref, nw_ref, win_ref, cr_ref, sr_ref, cm_ref, sm_ref, qan_ref, wqb_ref, kvan_ref,
                   wkvb_ref, qn_ref, kn_ref, qr_ref, kr_ref, vr_ref, g_ref, q_ref, k_ref, v_ref):
    h = _rms(x_ref[...], nw_ref[...]).astype(BF16)
    cr, sr = cr_ref[...], sr_ref[...]
    cm, sm = cm_ref[...], sm_ref[...]

    lat = _dot(h, win_ref[:, 4 * RET_WIDTH:4 * RET_WIDTH + 512])

    qk = _dot(h, win_ref[:, 0:2 * RET_WIDTH])
    for i in range(2 * RET_HEADS):
        seg = qk[:, i * LANE:(i + 1) * LANE]
        rot = seg * cr + pltpu.roll(seg, RET_DIM // 2, 1) * sr
        if i < RET_HEADS:
            qr_ref[:, i * LANE:(i + 1) * LANE] = (rot * (RET_DIM ** -0.5)).astype(BF16)
        else:
            j = i - RET_HEADS
            kr_ref[:, j * LANE:(j + 1) * LANE] = rot.astype(BF16)

    cq = _rms(lat[:, :MLA_Q_RANK], qan_ref[...]).astype(BF16)
    q4 = _dot(cq, wqb_ref[...])
    ckv = _rms(lat[:, MLA_Q_RANK:MLA_Q_RANK + MLA_KV_RANK], kvan_ref[...]).astype(BF16)
    kv4 = _dot(ckv, wkvb_ref[...])

    def rope64(t):
        return t * cm + pltpu.roll(t, MLA_ROPE // 2, 1) * sm

    krot = rope64(lat[:, MLA_Q_RANK + MLA_KV_RANK:])
    krot_sq = krot * krot
    qn, kn = qn_ref[...], kn_ref[...]
    for hh in range(MLA_HEADS):
        base = hh * MLA_QK_PAD
        nope = q4[:, base:base + LANE]
        rot = rope64(q4[:, base + LANE:base + 2 * LANE])
        ss = jnp.sum(nope * nope + rot * rot, axis=-1, keepdims=True)
        r = lax.rsqrt(ss * (1.0 / MLA_QK) + NORM_EPS) * (MLA_QK ** -0.5 * LOG2E)
        q_ref[:, base:base + LANE] = (nope * r * qn[:, :LANE]).astype(BF16)
        q_ref[:, base + LANE:base + 2 * LANE] = (rot * r * qn[:, LANE:]).astype(BF16)

        knope = kv4[:, base:base + LANE]
        rk = lax.rsqrt(jnp.sum(knope * knope + krot_sq, axis=-1, keepdims=True) * (1.0 / MLA_QK) + NORM_EPS)
        k_ref[:, base:base + LANE] = (knope * rk * kn[:, :LANE]).astype(BF16)
        k_ref[:, base + LANE:base + 2 * LANE] = (krot * rk * kn[:, LANE:]).astype(BF16)
        v_ref[:, hh * LANE:(hh + 1) * LANE] = kv4[:, base + LANE:base + 2 * LANE].astype(BF16)

    vg = _dot(h, win_ref[:, 2 * RET_WIDTH:4 * RET_WIDTH])
    vr_ref[...] = vg[:, :RET_WIDTH].astype(BF16)
    g_ref[...] = vg[:, RET_WIDTH:].astype(BF16)


def _hyb_in(x, p, tabs, seq, tm):
    t = x.shape[0]
    nps = seq // tm
    row = lambda w: pl.BlockSpec((tm, w), lambda i: (i, 0))
    tab = pl.BlockSpec((tm, LANE), lambda i: (i % nps, 0))
    out_w = (RET_WIDTH, RET_WIDTH, RET_WIDTH, RET_WIDTH, MLA_HEADS * MLA_QK_PAD, MLA_HEADS * MLA_QK_PAD,
             MLA_HEADS * MLA_V)
    return pl.pallas_call(
        _hyb_in_kernel,
        grid=(t // tm,),
        in_specs=[row(D_MODEL), _const_spec((1, D_MODEL)), _const_spec(p['w_in'].shape), tab, tab, tab, tab,
                  _const_spec((1, MLA_Q_RANK)), _const_spec(p['w_q_b'].shape), _const_spec((1, MLA_KV_RANK)),
                  _const_spec(p['w_kv_b'].shape), _const_spec((1, MLA_QK_PAD)), _const_spec((1, MLA_QK_PAD))],
        out_specs=[row(w) for w in out_w],
        out_shape=[jax.ShapeDtypeStruct((t, w), BF16) for w in out_w],
        compiler_params=pltpu.CompilerParams(dimension_semantics=("parallel",), vmem_limit_bytes=VMEM_LIMIT),
        name="hyb_in",
    )(x, p['hyb_norm_w'], p['w_in'], tabs['cr'], tabs['sr'], tabs['cm'], tabs['sm'], p['q_a_norm'], p['w_q_b'],
      p['kv_a_norm'], p['w_kv_b'], p['q_norm'], p['k_norm'])


RET_UNROLL = 16


def _ret_kernel(q_ref, k_ref, v_ref, g_ref, df_ref, db_ref, gn_ref, o_ref, dmat_ref, tab_ref, kv_ref, st_ref, *,
                chunk, seq):
    c = chunk
    n = seq // c
    lf = -jnp.exp(df_ref[...])
    lb = -jnp.exp(db_ref[...])
    row = lax.broadcasted_iota(jnp.int32, (c, c), 0).astype(F32)
    col = lax.broadcasted_iota(jnp.int32, (c, c), 1).astype(F32)
    diff = row - col
    dmat_ref[...] = jnp.where(diff >= 0, jnp.exp(lf * jnp.maximum(diff, 0.0)),
                              jnp.exp(lb * jnp.maximum(-diff, 0.0)))
    pos = lax.broadcasted_iota(jnp.int32, (c, LANE), 0).astype(F32)
    lf1, lb1 = lf[:, :LANE], lb[:, :LANE]
    tab_ref[0] = jnp.exp(lf1 * (pos + 1.0))
    tab_ref[1] = jnp.exp(lf1 * (c - 1.0 - pos))
    tab_ref[2] = jnp.exp(lb1 * (c - pos))
    tab_ref[3] = jnp.exp(lb1 * pos)
    gf = jnp.exp(lf1 * float(c))
    gb = jnp.exp(lb1 * float(c))

    def chunk_rows(i):
        return pl.ds(pl.multiple_of(i * c, c), c)

    def increments(i, carry):
        rows = chunk_rows(i)
        vc = v_ref[rows, :].astype(F32)
        v2 = jnp.concatenate([(vc * tab_ref[1]).astype(BF16), (vc * tab_ref[3]).astype(BF16)], axis=1)
        kv_ref[i] = _dot_tn(k_ref[rows, :], v2)
        return carry

    lax.fori_loop(0, n, increments, 0, unroll=RET_UNROLL)

    def scan_fwd(i, sf):
        st_ref[i, :RET_DIM, :] = sf.astype(BF16)
        return sf * gf + kv_ref[i, :, :RET_DIM]

    def scan_bwd(j, sb):
        i = n - 1 - j
        st_ref[i, RET_DIM:, :] = sb.astype(BF16)
        return sb * gb + kv_ref[i, :, RET_DIM:]

    zero_state = jnp.zeros((RET_DIM, RET_DIM), F32)
    lax.fori_loop(0, n, scan_fwd, zero_state)
    lax.fori_loop(0, n, scan_bwd, zero_state)

    def outputs(i, carry):
        rows = chunk_rows(i)
        qc = q_ref[rows, :]
        a = (_dot_nt(qc, k_ref[rows, :]) * dmat_ref[...]).astype(BF16)
        qf = qc.astype(F32)
        q2 = jnp.concatenate([(qf * tab_ref[0]).astype(BF16), (qf * tab_ref[2]).astype(BF16)], axis=1)
        o = _dot(a, v_ref[rows, :]) + _dot(q2, st_ref[i])
        y = _rms(o, gn_ref[...])
        o_ref[rows, :] = (y * _silu(g_ref[rows, :].astype(F32))).astype(BF16)
        return carry

    lax.fori_loop(0, n, outputs, 0, unroll=RET_UNROLL)


def _retention(qr, kr, vr, g, p, chunk):
    b, seq, _ = qr.shape
    n = seq // chunk
    head = pl.BlockSpec((None, seq, RET_DIM), lambda bi, hi: (bi, 0, hi))
    par = lambda w: pl.BlockSpec((None, 1, w), lambda bi, hi: (hi, 0, 0))
    return pl.pallas_call(
        functools.partial(_ret_kernel, chunk=chunk, seq=seq),
        grid=(b, RET_HEADS),
        in_specs=[head, head, head, head, par(chunk), par(chunk), par(RET_DIM)],
        out_specs=head,
        out_shape=jax.ShapeDtypeStruct((b, seq, RET_WIDTH), BF16),
        scratch_shapes=[pltpu.VMEM((chunk, chunk), F32),
                        pltpu.VMEM((4, chunk, LANE), F32),
                        pltpu.VMEM((n, RET_DIM, 2 * RET_DIM), F32),
                        pltpu.VMEM((n, 2 * RET_DIM, RET_DIM), BF16)],
        compiler_params=pltpu.CompilerParams(dimension_semantics=("parallel", "parallel"),
                                             vmem_limit_bytes=VMEM_LIMIT),
        name="retention",
    )(qr, kr, vr, g, p['decay_fwd'], p['decay_bwd'], p['gn_w'])


def _rows(start, size):
    if isinstance(start, int):
        return pl.ds(start, size)
    return pl.ds(pl.multiple_of(start, size), size)


def _attn_kernel(q_ref, k_ref, v_ref, o_ref, s0, s1, p0, p1, a0, a1, m_ref, acc_ref, v1_ref, *, tq, tk, seq, unroll):
    n = seq // tk
    total = (seq // tq) * n
    s_buf, p_buf, a_buf = (s0, s1), (p0, p1), (a0, a1)
    v1_ref[:, :MLA_V] = v_ref[...]
    v1_ref[:, MLA_V:] = jnp.ones((seq, MLA_V), BF16)
    m_ref[...] = jnp.full(m_ref.shape, -jnp.inf, F32)
    acc_ref[...] = jnp.zeros(acc_ref.shape, F32)

    def q_rows(t):
        return _rows((t // n) * tq, tq)

    def k_rows(t):
        return _rows((t % n) * tk, tk)

    def qk(t, slot):
        s_buf[slot][...] = _dot_nt(q_ref[q_rows(t), :], k_ref[k_rows(t), :])

    def softmax(t, slot):
        s = s_buf[slot][...]
        m_prev = jnp.where(t % n == 0, -jnp.inf, m_ref[...])
        m_new = jnp.maximum(m_prev, jnp.max(s, axis=1, keepdims=True))
        a_buf[slot][...] = jnp.exp2(m_prev - m_new)
        p_buf[slot][...] = jnp.exp2(s - jnp.concatenate([m_new] * (tk // LANE), axis=1)).astype(BF16)
        m_ref[...] = m_new

    def pv(t, slot):
        alpha = a_buf[slot][...]
        acc = acc_ref[...] * jnp.concatenate([alpha, alpha], axis=1) + _dot(p_buf[slot][...], v1_ref[k_rows(t), :])
        acc_ref[...] = acc
        o_ref[q_rows(t), :] = (acc[:, :MLA_V] / acc[:, MLA_V:]).astype(BF16)

    def step(t, slot, do_qk=True, do_softmax=True, do_pv=True):
        if do_qk:
            qk(t + 1, 1 - slot)
        if do_softmax:
            softmax(t, slot)
        if do_pv:
            pv(t - 1, 1 - slot)

    qk(0, 0)
    step(0, 0, do_pv=False)

    def body(i, carry):
        t0 = unroll * i + 1
        for u in range(unroll):
            step(t0 + u, (1 + u) % 2)
        return carry

    lax.fori_loop(0, (total - 2) // unroll, body, 0)
    step(total - 1, 1, do_qk=False)
    step(total, 0, do_qk=False, do_softmax=False)


def _attention(q, k, v, tq, tk, unroll):
    b, seq, _ = q.shape
    total = (seq // tq) * (seq // tk)
    assert unroll % 2 == 0 and (total - 2) % unroll == 0
    head = lambda w: pl.BlockSpec((None, seq, w), lambda bi, hi: (bi, 0, hi))
    return pl.pallas_call(
        functools.partial(_attn_kernel, tq=tq, tk=tk, seq=seq, unroll=unroll),
        grid=(b, MLA_HEADS),
        in_specs=[head(MLA_QK_PAD), head(MLA_QK_PAD), head(MLA_V)],
        out_specs=head(MLA_V),
        out_shape=jax.ShapeDtypeStruct((b, seq, MLA_HEADS * MLA_V), BF16),
        scratch_shapes=[pltpu.VMEM((tq, tk), F32), pltpu.VMEM((tq, tk), F32),
                        pltpu.VMEM((tq, tk), BF16), pltpu.VMEM((tq, tk), BF16),
                        pltpu.VMEM((tq, LANE), F32), pltpu.VMEM((tq, LANE), F32),
                        pltpu.VMEM((tq, LANE), F32),
                        pltpu.VMEM((tq, 2 * MLA_V), F32),
                        pltpu.VMEM((seq, 2 * MLA_V), BF16)],
        compiler_params=pltpu.CompilerParams(dimension_semantics=("parallel", "parallel"),
                                             vmem_limit_bytes=VMEM_LIMIT),
        name="attention",
    )(q, k, v)


FF_CHUNK = 256


def _ffn(x1, nw_ref, wg_ref, wu_ref, wd_ref, act_ref):
    h = _rms(x1, nw_ref[...]).astype(BF16)
    for c0 in range(0, D_FF, FF_CHUNK):
        g = _dot(h, wg_ref[:, c0:c0 + FF_CHUNK])
        u = _dot(h, wu_ref[:, c0:c0 + FF_CHUNK])
        act_ref[:, c0:c0 + FF_CHUNK] = (_silu(g) * u).astype(BF16)
    return x1 + _dot(act_ref[...], wd_ref[...])


def _ffn_specs():
    return [_const_spec((1, D_MODEL)), _const_spec((D_MODEL, D_FF)), _const_spec((D_MODEL, D_FF)),
            _const_spec((D_FF, D_MODEL))]


def _mix_ffn_kernel(x_ref, ret_ref, att_ref, wor_ref, woa_ref, nw_ref, wg_ref, wu_ref, wd_ref, o_ref, act_ref):
    x1 = x_ref[...] + _dot(ret_ref[...], wor_ref[...]) + _dot(att_ref[...], woa_ref[...])
    o_ref[...] = _ffn(x1, nw_ref, wg_ref, wu_ref, wd_ref, act_ref)


def _mix_ffn(x, ret, att, p, f, tm):
    t = x.shape[0]
    row = lambda w: pl.BlockSpec((tm, w), lambda i: (i, 0))
    return pl.pallas_call(
        _mix_ffn_kernel,
        grid=(t // tm,),
        in_specs=[row(D_MODEL), row(RET_WIDTH), row(MLA_HEADS * MLA_V), _const_spec((RET_WIDTH, D_MODEL)),
                  _const_spec((MLA_HEADS * MLA_V, D_MODEL))] + _ffn_specs(),
        out_specs=row(D_MODEL),
        out_shape=jax.ShapeDtypeStruct((t, D_MODEL), F32),
        scratch_shapes=[pltpu.VMEM((tm, D_FF), BF16)],
        compiler_params=pltpu.CompilerParams(dimension_semantics=("parallel",), vmem_limit_bytes=VMEM_LIMIT),
        name="mix_ffn",
    )(x, ret, att, p['w_out_ret'], p['w_out_att'], f['norm_w'], f['w_gate'], f['w_up'], f['w_down'])


CONV_ROWS = 64
N_FF_CHUNKS = D_FF // FF_CHUNK


def _conv_unit(hbuf_ref, wdw_ref, cbuf_ref, r0, l0):
    lanes = slice(l0, l0 + LANE)
    acc = None
    for sh in range(SUBLANE):
        part = None
        for kk in range(CONV_WIDTH):
            off = HALO - CONV_PAD + kk
            if off % SUBLANE != sh:
                continue
            al = r0 + off - sh
            t = hbuf_ref[al:al + CONV_ROWS + SUBLANE, lanes] * wdw_ref[kk:kk + 1, lanes]
            part = t if part is None else part + t
        part = part[sh:sh + CONV_ROWS, :]
        acc = part if acc is None else acc + part
    cbuf_ref[r0:r0 + CONV_ROWS, lanes] = acc


def _conv_ffn_kernel(xp_ref, x_ref, xn_ref, xprev_ref, nw_ref, w1_ref, b1_ref, wdw_ref, bdw_ref, mw_ref, w2_ref,
                     b2_ref, fnw_ref, wg_ref, wu_ref, wd_ref, o_ref, xcat_ref, hbuf_ref, cbuf_ref, hn_ref, x1_ref,
                     act_ref, *, tm, nps, nt):
    i = pl.program_id(0)
    ti = jnp.minimum(i, nt - 1)
    first = (ti % nps) == 0
    last = (ti % nps) == nps - 1

    @pl.when(i == 0)
    def _():
        cbuf_ref[...] = jnp.zeros(cbuf_ref.shape, F32)

    h2 = _silu(_rms(cbuf_ref[...] + bdw_ref[...], mw_ref[...])).astype(BF16)
    x1 = xprev_ref[...] + _dot(h2, w2_ref[...]) + b2_ref[...]
    x1_ref[...] = x1
    hn_ref[...] = _rms(x1, fnw_ref[...]).astype(BF16)

    nw = nw_ref[...]
    xcat_ref[0:HALO, :] = _rms(xp_ref[...], nw).astype(BF16)
    xcat_ref[HALO:HALO + tm, :] = _rms(x_ref[...], nw).astype(BF16)
    xcat_ref[HALO + tm:, :] = _rms(xn_ref[...], nw).astype(BF16)
    hcat = xcat_ref[...]
    a = _dot(hcat, w1_ref[:, :D_MODEL]) + b1_ref[:, :D_MODEL]
    gate = _dot(hcat, w1_ref[:, D_MODEL:]) + b1_ref[:, D_MODEL:]
    hbuf_ref[...] = a / (1.0 + jnp.exp(-gate))

    @pl.when(first)
    def _():
        hbuf_ref[0:HALO, :] = jnp.zeros((HALO, D_MODEL), F32)

    @pl.when(last)
    def _():
        hbuf_ref[HALO + tm:, :] = jnp.zeros((HALO, D_MODEL), F32)

    lane_blocks = D_MODEL // LANE
    units = [((u // lane_blocks) * CONV_ROWS, (u % lane_blocks) * LANE) for u in range((tm // CONV_ROWS) * lane_blocks)]
    per_dot = -(-len(units) // (2 * N_FF_CHUNKS))

    def conv_units(k):
        for r0, l0 in units[k * per_dot:(k + 1) * per_dot]:
            _conv_unit(hbuf_ref, wdw_ref, cbuf_ref, r0, l0)

    hn = hn_ref[...]
    for c in range(N_FF_CHUNKS):
        cols = slice(c * FF_CHUNK, (c + 1) * FF_CHUNK)
        g = _dot(hn, wg_ref[:, cols])
        conv_units(2 * c)
        u = _dot(hn, wu_ref[:, cols])
        act_ref[:, cols] = (_silu(g) * u).astype(BF16)
        conv_units(2 * c + 1)
    o_ref[...] = x1_ref[...] + _dot(act_ref[...], wd_ref[...])


def _conv_ffn(x, p, f, seq, tm):
    t = x.shape[0]
    nps, nt = seq // tm, t // tm
    hb = tm // HALO
    x3 = x.reshape(t // HALO, HALO, D_MODEL)
    nblk = t // HALO
    return pl.pallas_call(
        functools.partial(_conv_ffn_kernel, tm=tm, nps=nps, nt=nt),
        grid=(nt + 1,),
        in_specs=[pl.BlockSpec((None, HALO, D_MODEL), lambda i: (jnp.clip(i * hb - 1, 0, nblk - 1), 0, 0)),
                  pl.BlockSpec((tm, D_MODEL), lambda i: (jnp.minimum(i, nt - 1), 0)),
                  pl.BlockSpec((None, HALO, D_MODEL), lambda i: (jnp.minimum((i + 1) * hb, nblk - 1), 0, 0)),
                  pl.BlockSpec((tm, D_MODEL), lambda i: (jnp.maximum(i - 1, 0), 0)),
                  _const_spec((1, D_MODEL)), _const_spec((D_MODEL, 2 * D_MODEL)), _const_spec((1, 2 * D_MODEL)),
                  _const_spec((CONV_WIDTH, D_MODEL)), _const_spec((1, D_MODEL)), _const_spec((1, D_MODEL)),
                  _const_spec((D_MODEL, D_MODEL)), _const_spec((1, D_MODEL))] + _ffn_specs(),
        out_specs=pl.BlockSpec((tm, D_MODEL), lambda i: (jnp.maximum(i - 1, 0), 0)),
        out_shape=jax.ShapeDtypeStruct((t, D_MODEL), F32),
        scratch_shapes=[pltpu.VMEM((tm + 2 * HALO, D_MODEL), BF16),
                        pltpu.VMEM((tm + 2 * HALO, D_MODEL), F32),
                        pltpu.VMEM((tm, D_MODEL), F32),
                        pltpu.VMEM((tm, D_MODEL), BF16),
                        pltpu.VMEM((tm, D_MODEL), F32),
                        pltpu.VMEM((tm, D_FF), BF16)],
        compiler_params=pltpu.CompilerParams(dimension_semantics=("arbitrary",), vmem_limit_bytes=VMEM_LIMIT),
        name="conv_ffn",
    )(x3, x, x3, x, p['norm_w'], p['w_pw1'], p['b_pw1'], p['w_dw'], p['b_dw'], p['mid_norm_w'], p['w_pw2'],
      p['b_pw2'], f['norm_w'], f['w_gate'], f['w_up'], f['w_down'])


def _rope_tables(seq):
    def cos_sin(half):
        inv = ROPE_BASE ** (-jnp.arange(half, dtype=F32) / half)
        ang = jnp.arange(seq, dtype=F32)[:, None] * inv[None, :]
        return jnp.cos(ang), jnp.sin(ang)

    c, s = cos_sin(RET_DIM // 2)
    cm, sm = cos_sin(MLA_ROPE // 2)
    z = jnp.zeros_like(cm)
    return {'cr': jnp.concatenate([c, c], axis=1), 'sr': jnp.concatenate([-s, s], axis=1),
            'cm': jnp.concatenate([cm, cm, z, z], axis=1), 'sm': jnp.concatenate([-sm, sm, z, z], axis=1)}


def _prep_hybrid(norm_w, w_in, decay_fwd, decay_bwd, gn_w, w_q_b, q_a_norm, w_kv_b, kv_a_norm, q_norm, k_norm,
                 w_out, chunk):
    kpe = w_in[:, -MLA_ROPE:]
    wq = w_q_b.reshape(MLA_Q_RANK, MLA_HEADS, MLA_QK)
    wq = jnp.concatenate([wq, wq[:, :, MLA_NOPE:]], axis=-1).reshape(MLA_Q_RANK, MLA_HEADS * MLA_QK_PAD)
    pad = lambda w: jnp.concatenate([w, jnp.zeros((MLA_QK_PAD - MLA_QK,), F32)])[None, :]
    bc = lambda d: jnp.broadcast_to(d[:, None, None], (RET_HEADS, 1, chunk))
    return {
        'hyb_norm_w': norm_w[None, :], 'w_in': jnp.concatenate([w_in, kpe], axis=1).astype(BF16),
        'decay_fwd': bc(decay_fwd), 'decay_bwd': bc(decay_bwd), 'gn_w': gn_w[:, None, :],
        'w_q_b': wq.astype(BF16), 'q_a_norm': q_a_norm[None, :], 'w_kv_b': w_kv_b.astype(BF16),
        'kv_a_norm': kv_a_norm[None, :], 'q_norm': pad(q_norm), 'k_norm': pad(k_norm),
        'w_out_ret': w_out[:RET_WIDTH].astype(BF16), 'w_out_att': w_out[RET_WIDTH:].astype(BF16),
    }


def _prep_conv(norm_w, w_pw1, b_pw1, w_dw, b_dw, mid_norm_w, w_pw2, b_pw2):
    return {'norm_w': norm_w[None, :], 'w_pw1': w_pw1.astype(BF16), 'b_pw1': b_pw1[None, :], 'w_dw': w_dw,
            'b_dw': b_dw[None, :], 'mid_norm_w': mid_norm_w[None, :], 'w_pw2': w_pw2.astype(BF16),
            'b_pw2': b_pw2[None, :]}


def _prep_ffn(norm_w, w_gate, w_up, w_down):
    return {'norm_w': norm_w[None, :], 'w_gate': w_gate.astype(BF16), 'w_up': w_up.astype(BF16),
            'w_down': w_down.astype(BF16)}


MAX_ATTN_UNROLL = 14


def _tiles(seq):
    tm = min(512, seq)
    tq, tk = min(256, seq), min(2048, seq)
    steady = (seq // tq) * (seq // tk) - 2
    unroll = max([r for r in range(2, MAX_ATTN_UNROLL + 1, 2) if steady % r == 0], default=2)
    return {'tm': tm, 'chunk': min(256, seq), 'tq': tq, 'tk': tk, 'attn_unroll': unroll}


def _run_group(x, hyb, conv, ffn0, ffn1, cfg):
    b, seq, _ = x.shape
    xf = x.reshape(b * seq, D_MODEL)
    tabs = _rope_tables(seq)
    qr, kr, vr, g, q, k, v = _hyb_in(xf, hyb, tabs, seq, cfg['tm'])
    r3 = lambda a: a.reshape(b, seq, a.shape[-1])
    ret = _retention(r3(qr), r3(kr), r3(vr), r3(g), hyb, cfg['chunk'])
    att = _attention(r3(q), r3(k), r3(v), cfg['tq'], cfg['tk'], cfg['attn_unroll'])
    x1 = _mix_ffn(xf, ret.reshape(b * seq, -1), att.reshape(b * seq, -1), hyb, ffn0, cfg['tm'])
    x2 = _conv_ffn(x1, conv, ffn1, seq, cfg['tm'])
    return x2.reshape(b, seq, D_MODEL)


def kernel(x_prompt, x_sample, hyb_norm_w, hyb_w_in, ret_decay_fwd, ret_decay_bwd, ret_gn_w, mla_w_q_b, mla_q_a_norm, mla_w_kv_b, mla_kv_a_norm, mla_q_norm, mla_k_norm, hyb_w_out, conv_norm_w, conv_w_pw1, conv_b_pw1, conv_w_dw, conv_b_dw, conv_mid_norm_w, conv_w_pw2, conv_b_pw2, ffn_norm_w, ffn_w_gate, ffn_w_up, ffn_w_down):
    conv = _prep_conv(conv_norm_w[0], conv_w_pw1[0], conv_b_pw1[0], conv_w_dw[0], conv_b_dw[0],
                      conv_mid_norm_w[0], conv_w_pw2[0], conv_b_pw2[0])
    ffn0 = _prep_ffn(ffn_norm_w[0], ffn_w_gate[0], ffn_w_up[0], ffn_w_down[0])
    ffn1 = _prep_ffn(ffn_norm_w[1], ffn_w_gate[1], ffn_w_up[1], ffn_w_down[1])

    def run(x):
        cfg = _tiles(x.shape[1])
        hyb = _prep_hybrid(hyb_norm_w[0], hyb_w_in[0], ret_decay_fwd[0], ret_decay_bwd[0], ret_gn_w[0],
                           mla_w_q_b[0], mla_q_a_norm[0], mla_w_kv_b[0], mla_kv_a_norm[0], mla_q_norm[0],
                           mla_k_norm[0], hyb_w_out[0], cfg['chunk'])
        return _run_group(x, hyb, conv, ffn0, ffn1, cfg)

    return (run(x_prompt), run(x_sample))
```

```python
import functools
import math

import jax
import jax.numpy as jnp
from jax import lax
from jax.experimental import pallas as pl
from jax.experimental.pallas import tpu as pltpu

D_MODEL = 1024
RET_HEADS = 4
RET_DIM = 128
RET_WIDTH = RET_HEADS * RET_DIM
MLA_HEADS = 4
MLA_Q_RANK = 256
MLA_KV_RANK = 128
MLA_NOPE = 128
MLA_ROPE = 64
MLA_V = 128
MLA_QK = MLA_NOPE + MLA_ROPE
MLA_QK_PAD = 256
ROPE_BASE = 10000.0
CONV_WIDTH = 31
CONV_PAD = (CONV_WIDTH - 1) // 2
HALO = 16
D_FF = 2816
NORM_EPS = 1e-6
LOG2E = math.log2(math.e)

LANE = 128
SUBLANE = 8
VMEM_LIMIT = 56 * 1024 * 1024

F32 = jnp.float32
BF16 = jnp.bfloat16


def _dot(a, b):
    return jnp.dot(a, b, preferred_element_type=F32)


def _dot_nt(a, b):
    return lax.dot_general(a, b, (((1,), (1,)), ((), ())), preferred_element_type=F32)


def _dot_tn(a, b):
    return lax.dot_general(a, b, (((0,), (0,)), ((), ())), preferred_element_type=F32)


def _rms(x, w):
    return x * lax.rsqrt(jnp.mean(x * x, axis=-1, keepdims=True) + NORM_EPS) * w


def _silu(g):
    return g / (1.0 + jnp.exp(-g))


def _const_spec(shape):
    return pl.BlockSpec(shape, lambda *_: (0,) * len(shape), pipeline_mode=pl.Buffered(1))


def _hyb_in_kernel(x_ref, nw_ref, win_ref, cr_ref, sr_ref, cm_ref, sm_ref, qan_ref, wqb_ref, kvan_ref,
                   wkvb_ref, qn_ref, kn_ref, qr_ref, kr_ref, vr_ref, g_ref, q_ref, k_ref, v_ref):
    h = _rms(x_ref[...], nw_ref[...]).astype(BF16)
    cr, sr = cr_ref[...], sr_ref[...]
    cm, sm = cm_ref[...], sm_ref[...]

    lat = _dot(h, win_ref[:, 4 * RET_WIDTH:4 * RET_WIDTH + 512])

    qk = _dot(h, win_ref[:, 0:2 * RET_WIDTH])
    for i in range(2 * RET_HEADS):
        seg = qk[:, i * LANE:(i + 1) * LANE]
        rot = seg * cr + pltpu.roll(seg, RET_DIM // 2, 1) * sr
        if i < RET_HEADS:
            qr_ref[:, i * LANE:(i + 1) * LANE] = (rot * (RET_DIM ** -0.5)).astype(BF16)
        else:
            j = i - RET_HEADS
            kr_ref[:, j * LANE:(j + 1) * LANE] = rot.astype(BF16)

    cq = _rms(lat[:, :MLA_Q_RANK], qan_ref[...]).astype(BF16)
    q4 = _dot(cq, wqb_ref[...])
    ckv = _rms(lat[:, MLA_Q_RANK:MLA_Q_RANK + MLA_KV_RANK], kvan_ref[...]).astype(BF16)
    kv4 = _dot(ckv, wkvb_ref[...])

    def rope64(t):
        return t * cm + pltpu.roll(t, MLA_ROPE // 2, 1) * sm

    krot = rope64(lat[:, MLA_Q_RANK + MLA_KV_RANK:])
    krot_sq = krot * krot
    qn, kn = qn_ref[...], kn_ref[...]
    for hh in range(MLA_HEADS):
        base = hh * MLA_QK_PAD
        nope = q4[:, base:base + LANE]
        rot = rope64(q4[:, base + LANE:base + 2 * LANE])
        ss = jnp.sum(nope * nope + rot * rot, axis=-1, keepdims=True)
        r = lax.rsqrt(ss * (1.0 / MLA_QK) + NORM_EPS) * (MLA_QK ** -0.5 * LOG2E)
        q_ref[:, base:base + LANE] = (nope * r * qn[:, :LANE]).astype(BF16)
        q_ref[:, base + LANE:base + 2 * LANE] = (rot * r * qn[:, LANE:]).astype(BF16)

        knope = kv4[:, base:base + LANE]
        rk = lax.rsqrt(jnp.sum(knope * knope + krot_sq, axis=-1, keepdims=True) * (1.0 / MLA_QK) + NORM_EPS)
        k_ref[:, base:base + LANE] = (knope * rk * kn[:, :LANE]).astype(BF16)
        k_ref[:, base + LANE:base + 2 * LANE] = (krot * rk * kn[:, LANE:]).astype(BF16)
        v_ref[:, hh * LANE:(hh + 1) * LANE] = kv4[:, base + LANE:base + 2 * LANE].astype(BF16)

    vg = _dot(h, win_ref[:, 2 * RET_WIDTH:4 * RET_WIDTH])
    vr_ref[...] = vg[:, :RET_WIDTH].astype(BF16)
    g_ref[...] = vg[:, RET_WIDTH:].astype(BF16)


def _hyb_in(x, p, tabs, seq, tm):
    t = x.shape[0]
    nps = seq // tm
    row = lambda w: pl.BlockSpec((tm, w), lambda i: (i, 0))
    tab = pl.BlockSpec((tm, LANE), lambda i: (i % nps, 0))
    out_w = (RET_WIDTH, RET_WIDTH, RET_WIDTH, RET_WIDTH, MLA_HEADS * MLA_QK_PAD, MLA_HEADS * MLA_QK_PAD,
             MLA_HEADS * MLA_V)
    return pl.pallas_call(
        _hyb_in_kernel,
        grid=(t // tm,),
        in_specs=[row(D_MODEL), _const_spec((1, D_MODEL)), _const_spec(p['w_in'].shape), tab, tab, tab, tab,
                  _const_spec((1, MLA_Q_RANK)), _const_spec(p['w_q_b'].shape), _const_spec((1, MLA_KV_RANK)),
                  _const_spec(p['w_kv_b'].shape), _const_spec((1, MLA_QK_PAD)), _const_spec((1, MLA_QK_PAD))],
        out_specs=[row(w) for w in out_w],
        out_shape=[jax.ShapeDtypeStruct((t, w), BF16) for w in out_w],
        compiler_params=pltpu.CompilerParams(dimension_semantics=("parallel",), vmem_limit_bytes=VMEM_LIMIT),
        name="hyb_in",
    )(x, p['hyb_norm_w'], p['w_in'], tabs['cr'], tabs['sr'], tabs['cm'], tabs['sm'], p['q_a_norm'], p['w_q_b'],
      p['kv_a_norm'], p['w_kv_b'], p['q_norm'], p['k_norm'])


RET_UNROLL = 16


def _ret_kernel(q_ref, k_ref, v_ref, g_ref, df_ref, db_ref, gn_ref, o_ref, dmat_ref, tab_ref, kv_ref, st_ref, *,
                chunk, seq):
    c = chunk
    n = seq // c
    lf = -jnp.exp(df_ref[...])
    lb = -jnp.exp(db_ref[...])
    row = lax.broadcasted_iota(jnp.int32, (c, c), 0).astype(F32)
    col = lax.broadcasted_iota(jnp.int32, (c, c), 1).astype(F32)
    diff = row - col
    dmat_ref[...] = jnp.where(diff >= 0, jnp.exp(lf * jnp.maximum(diff, 0.0)),
                              jnp.exp(lb * jnp.maximum(-diff, 0.0)))
    pos = lax.broadcasted_iota(jnp.int32, (c, LANE), 0).astype(F32)
    lf1, lb1 = lf[:, :LANE], lb[:, :LANE]
    tab_ref[0] = jnp.exp(lf1 * (pos + 1.0))
    tab_ref[1] = jnp.exp(lf1 * (c - 1.0 - pos))
    tab_ref[2] = jnp.exp(lb1 * (c - pos))
    tab_ref[3] = jnp.exp(lb1 * pos)
    gf = jnp.exp(lf1 * float(c))
    gb = jnp.exp(lb1 * float(c))

    def chunk_rows(i):
        return pl.ds(pl.multiple_of(i * c, c), c)

    def increments(i, carry):
        rows = chunk_rows(i)
        vc = v_ref[rows, :].astype(F32)
        v2 = jnp.concatenate([(vc * tab_ref[1]).astype(BF16), (vc * tab_ref[3]).astype(BF16)], axis=1)
        kv_ref[i] = _dot_tn(k_ref[rows, :], v2)
        return carry

    lax.fori_loop(0, n, increments, 0, unroll=RET_UNROLL)

    def scan_fwd(i, sf):
        st_ref[i, :RET_DIM, :] = sf.astype(BF16)
        return sf * gf + kv_ref[i, :, :RET_DIM]

    def scan_bwd(j, sb):
        i = n - 1 - j
        st_ref[i, RET_DIM:, :] = sb.astype(BF16)
        return sb * gb + kv_ref[i, :, RET_DIM:]

    zero_state = jnp.zeros((RET_DIM, RET_DIM), F32)
    lax.fori_loop(0, n, scan_fwd, zero_state)
    lax.fori_loop(0, n, scan_bwd, zero_state)

    def outputs(i, carry):
        rows = chunk_rows(i)
        qc = q_ref[rows, :]
        a = (_dot_nt(qc, k_ref[rows, :]) * dmat_ref[...]).astype(BF16)
        qf = qc.astype(F32)
        q2 = jnp.concatenate([(qf * tab_ref[0]).astype(BF16), (qf * tab_ref[2]).astype(BF16)], axis=1)
        o = _dot(a, v_ref[rows, :]) + _dot(q2, st_ref[i])
        y = _rms(o, gn_ref[...])
        o_ref[rows, :] = (y * _silu(g_ref[rows, :].astype(F32))).astype(BF16)
        return carry

    lax.fori_loop(0, n, outputs, 0, unroll=RET_UNROLL)


def _retention(qr, kr, vr, g, p, chunk):
    b, seq, _ = qr.shape
    n = seq // chunk
    head = pl.BlockSpec((None, seq, RET_DIM), lambda bi, hi: (bi, 0, hi))
    par = lambda w: pl.BlockSpec((None, 1, w), lambda bi, hi: (hi, 0, 0))
    return pl.pallas_call(
        functools.partial(_ret_kernel, chunk=chunk, seq=seq),
        grid=(b, RET_HEADS),
        in_specs=[head, head, head, head, par(chunk), par(chunk), par(RET_DIM)],
        out_specs=head,
        out_shape=jax.ShapeDtypeStruct((b, seq, RET_WIDTH), BF16),
        scratch_shapes=[pltpu.VMEM((chunk, chunk), F32),
                        pltpu.VMEM((4, chunk, LANE), F32),
                        pltpu.VMEM((n, RET_DIM, 2 * RET_DIM), F32),
                        pltpu.VMEM((n, 2 * RET_DIM, RET_DIM), BF16)],
        compiler_params=pltpu.CompilerParams(dimension_semantics=("parallel", "parallel"),
                                             vmem_limit_bytes=VMEM_LIMIT),
        name="retention",
    )(qr, kr, vr, g, p['decay_fwd'], p['decay_bwd'], p['gn_w'])


def _rows(start, size):
    if isinstance(start, int):
        return pl.ds(start, size)
    return pl.ds(pl.multiple_of(start, size), size)


def _attn_kernel(q_ref, k_ref, v_ref, o_ref, s0, s1, p0, p1, a0, a1, m_ref, acc_ref, v1_ref, *, tq, tk, seq, unroll):
    n = seq // tk
    total = (seq // tq) * n
    s_buf, p_buf, a_buf = (s0, s1), (p0, p1), (a0, a1)
    v1_ref[:, :MLA_V] = v_ref[...]
    v1_ref[:, MLA_V:] = jnp.ones((seq, MLA_V), BF16)
    m_ref[...] = jnp.full(m_ref.shape, -jnp.inf, F32)
    acc_ref[...] = jnp.zeros(acc_ref.shape, F32)

    def q_rows(t):
        return _rows((t // n) * tq, tq)

    def k_rows(t):
        return _rows((t % n) * tk, tk)

    def qk(t, slot):
        s_buf[slot][...] = _dot_nt(q_ref[q_rows(t), :], k_ref[k_rows(t), :])

    def softmax(t, slot):
        s = s_buf[slot][...]
        m_prev = jnp.where(t % n == 0, -jnp.inf, m_ref[...])
        m_new = jnp.maximum(m_prev, jnp.max(s, axis=1, keepdims=True))
        a_buf[slot][...] = jnp.exp2(m_prev - m_new)
        p_buf[slot][...] = jnp.exp2(s - jnp.concatenate([m_new] * (tk // LANE), axis=1)).astype(BF16)
        m_ref[...] = m_new

    def pv(t, slot):
        alpha = a_buf[slot][...]
        acc = acc_ref[...] * jnp.concatenate([alpha, alpha], axis=1) + _dot(p_buf[slot][...], v1_ref[k_rows(t), :])
        acc_ref[...] = acc
        o_ref[q_rows(t), :] = (acc[:, :MLA_V] / acc[:, MLA_V:]).astype(BF16)

    def step(t, slot, do_qk=True, do_softmax=True, do_pv=True):
        if do_qk:
            qk(t + 1, 1 - slot)
        if do_softmax:
            softmax(t, slot)
        if do_pv:
            pv(t - 1, 1 - slot)

    qk(0, 0)
    step(0, 0, do_pv=False)

    def body(i, carry):
        t0 = unroll * i + 1
        for u in range(unroll):
            step(t0 + u, (1 + u) % 2)
        return carry

    lax.fori_loop(0, (total - 2) // unroll, body, 0)
    step(total - 1, 1, do_qk=False)
    step(total, 0, do_qk=False, do_softmax=False)


def _attention(q, k, v, tq, tk, unroll):
    b, seq, _ = q.shape
    total = (seq // tq) * (seq // tk)
    assert unroll % 2 == 0 and (total - 2) % unroll == 0
    head = lambda w: pl.BlockSpec((None, seq, w), lambda bi, hi: (bi, 0, hi))
    return pl.pallas_call(
        functools.partial(_attn_kernel, tq=tq, tk=tk, seq=seq, unroll=unroll),
        grid=(b, MLA_HEADS),
        in_specs=[head(MLA_QK_PAD), head(MLA_QK_PAD), head(MLA_V)],
        out_specs=head(MLA_V),
        out_shape=jax.ShapeDtypeStruct((b, seq, MLA_HEADS * MLA_V), BF16),
        scratch_shapes=[pltpu.VMEM((tq, tk), F32), pltpu.VMEM((tq, tk), F32),
                        pltpu.VMEM((tq, tk), BF16), pltpu.VMEM((tq, tk), BF16),
                        pltpu.VMEM((tq, LANE), F32), pltpu.VMEM((tq, LANE), F32),
                        pltpu.VMEM((tq, LANE), F32),
                        pltpu.VMEM((tq, 2 * MLA_V), F32),
                        pltpu.VMEM((seq, 2 * MLA_V), BF16)],
        compiler_params=pltpu.CompilerParams(dimension_semantics=("parallel", "parallel"),
                                             vmem_limit_bytes=VMEM_LIMIT),
        name="attention",
    )(q, k, v)


FF_CHUNK = 256


def _ffn(x1, nw_ref, wg_ref, wu_ref, wd_ref, act_ref):
    h = _rms(x1, nw_ref[...]).astype(BF16)
    for c0 in range(0, D_FF, FF_CHUNK):
        g = _dot(h, wg_ref[:, c0:c0 + FF_CHUNK])
        u = _dot(h, wu_ref[:, c0:c0 + FF_CHUNK])
        act_ref[:, c0:c0 + FF_CHUNK] = (_silu(g) * u).astype(BF16)
    return x1 + _dot(act_ref[...], wd_ref[...])


def _ffn_specs():
    return [_const_spec((1, D_MODEL)), _const_spec((D_MODEL, D_FF)), _const_spec((D_MODEL, D_FF)),
            _const_spec((D_FF, D_MODEL))]


def _mix_ffn_kernel(x_ref, ret_ref, att_ref, wor_ref, woa_ref, nw_ref, wg_ref, wu_ref, wd_ref, o_ref, act_ref):
    x1 = x_ref[...] + _dot(ret_ref[...], wor_ref[...]) + _dot(att_ref[...], woa_ref[...])
    o_ref[...] = _ffn(x1, nw_ref, wg_ref, wu_ref, wd_ref, act_ref)


def _mix_ffn(x, ret, att, p, f, tm):
    t = x.shape[0]
    row = lambda w: pl.BlockSpec((tm, w), lambda i: (i, 0))
    return pl.pallas_call(
        _mix_ffn_kernel,
        grid=(t // tm,),
        in_specs=[row(D_MODEL), row(RET_WIDTH), row(MLA_HEADS * MLA_V), _const_spec((RET_WIDTH, D_MODEL)),
                  _const_spec((MLA_HEADS * MLA_V, D_MODEL))] + _ffn_specs(),
        out_specs=row(D_MODEL),
        out_shape=jax.ShapeDtypeStruct((t, D_MODEL), F32),
        scratch_shapes=[pltpu.VMEM((tm, D_FF), BF16)],
        compiler_params=pltpu.CompilerParams(dimension_semantics=("parallel",), vmem_limit_bytes=VMEM_LIMIT),
        name="mix_ffn",
    )(x, ret, att, p['w_out_ret'], p['w_out_att'], f['norm_w'], f['w_gate'], f['w_up'], f['w_down'])


CONV_ROWS = 64
N_FF_CHUNKS = D_FF // FF_CHUNK


def _conv_unit(hbuf_ref, wdw_ref, cbuf_ref, r0, l0):
    lanes = slice(l0, l0 + LANE)
    acc = None
    for sh in range(SUBLANE):
        part = None
        for kk in range(CONV_WIDTH):
            off = HALO - CONV_PAD + kk
            if off % SUBLANE != sh:
                continue
            al = r0 + off - sh
            t = hbuf_ref[al:al + CONV_ROWS + SUBLANE, lanes] * wdw_ref[kk:kk + 1, lanes]
            part = t if part is None else part + t
        part = part[sh:sh + CONV_ROWS, :]
        acc = part if acc is None else acc + part
    cbuf_ref[r0:r0 + CONV_ROWS, lanes] = acc


def _conv_ffn_kernel(xp_ref, x_ref, xn_ref, xprev_ref, nw_ref, w1_ref, b1_ref, wdw_ref, bdw_ref, mw_ref, w2_ref,
                     b2_ref, fnw_ref, wg_ref, wu_ref, wd_ref, o_ref, xcat_ref, hbuf_ref, cbuf_ref, hn_ref, x1_ref,
                     act_ref, *, tm, nps, nt):
    i = pl.program_id(0)
    ti = jnp.minimum(i, nt - 1)
    first = (ti % nps) == 0
    last = (ti % nps) == nps - 1

    @pl.when(i == 0)
    def _():
        cbuf_ref[...] = jnp.zeros(cbuf_ref.shape, F32)

    h2 = _silu(_rms(cbuf_ref[...] + bdw_ref[...], mw_ref[...])).astype(BF16)
    x1 = xprev_ref[...] + _dot(h2, w2_ref[...]) + b2_ref[...]
    x1_ref[...] = x1
    hn_ref[...] = _rms(x1, fnw_ref[...]).astype(BF16)

    nw = nw_ref[...]
    xcat_ref[0:HALO, :] = _rms(xp_ref[...], nw).astype(BF16)
    xcat_ref[HALO:HALO + tm, :] = _rms(x_ref[...], nw).astype(BF16)
    xcat_ref[HALO + tm:, :] = _rms(xn_ref[...], nw).astype(BF16)
    hcat = xcat_ref[...]
    a = _dot(hcat, w1_ref[:, :D_MODEL]) + b1_ref[:, :D_MODEL]
    gate = _dot(hcat, w1_ref[:, D_MODEL:]) + b1_ref[:, D_MODEL:]
    hbuf_ref[...] = a / (1.0 + jnp.exp(-gate))

    @pl.when(first)
    def _():
        hbuf_ref[0:HALO, :] = jnp.zeros((HALO, D_MODEL), F32)

    @pl.when(last)
    def _():
        hbuf_ref[HALO + tm:, :] = jnp.zeros((HALO, D_MODEL), F32)

    lane_blocks = D_MODEL // LANE
    units = [((u // lane_blocks) * CONV_ROWS, (u % lane_blocks) * LANE) for u in range((tm // CONV_ROWS) * lane_blocks)]
    per_dot = -(-len(units) // (2 * N_FF_CHUNKS))

    def conv_units(k):
        for r0, l0 in units[k * per_dot:(k + 1) * per_dot]:
            _conv_unit(hbuf_ref, wdw_ref, cbuf_ref, r0, l0)

    hn = hn_ref[...]
    for c in range(N_FF_CHUNKS):
        cols = slice(c * FF_CHUNK, (c + 1) * FF_CHUNK)
        g = _dot(hn, wg_ref[:, cols])
        conv_units(2 * c)
        u = _dot(hn, wu_ref[:, cols])
        act_ref[:, cols] = (_silu(g) * u).astype(BF16)
        conv_units(2 * c + 1)
    o_ref[...] = x1_ref[...] + _dot(act_ref[...], wd_ref[...])


def _conv_ffn(x, p, f, seq, tm):
    t = x.shape[0]
    nps, nt = seq // tm, t // tm
    hb = tm // HALO
    x3 = x.reshape(t // HALO, HALO, D_MODEL)
    nblk = t // HALO
    return pl.pallas_call(
        functools.partial(_conv_ffn_kernel, tm=tm, nps=nps, nt=nt),
        grid=(nt + 1,),
        in_specs=[pl.BlockSpec((None, HALO, D_MODEL), lambda i: (jnp.clip(i * hb - 1, 0, nblk - 1), 0, 0)),
                  pl.BlockSpec((tm, D_MODEL), lambda i: (jnp.minimum(i, nt - 1), 0)),
                  pl.BlockSpec((None, HALO, D_MODEL), lambda i: (jnp.minimum((i + 1) * hb, nblk - 1), 0, 0)),
                  pl.BlockSpec((tm, D_MODEL), lambda i: (jnp.maximum(i - 1, 0), 0)),
                  _const_spec((1, D_MODEL)), _const_spec((D_MODEL, 2 * D_MODEL)), _const_spec((1, 2 * D_MODEL)),
                  _const_spec((CONV_WIDTH, D_MODEL)), _const_spec((1, D_MODEL)), _const_spec((1, D_MODEL)),
                  _const_spec((D_MODEL, D_MODEL)), _const_spec((1, D_MODEL))] + _ffn_specs(),
        out_specs=pl.BlockSpec((tm, D_MODEL), lambda i: (jnp.maximum(i - 1, 0), 0)),
        out_shape=jax.ShapeDtypeStruct((t, D_MODEL), F32),
        scratch_shapes=[pltpu.VMEM((tm + 2 * HALO, D_MODEL), BF16),
                        pltpu.VMEM((tm + 2 * HALO, D_MODEL), F32),
                        pltpu.VMEM((tm, D_MODEL), F32),
                        pltpu.VMEM((tm, D_MODEL), BF16),
                        pltpu.VMEM((tm, D_MODEL), F32),
                        pltpu.VMEM((tm, D_FF), BF16)],
        compiler_params=pltpu.CompilerParams(dimension_semantics=("arbitrary",), vmem_limit_bytes=VMEM_LIMIT),
        name="conv_ffn",
    )(x3, x, x3, x, p['norm_w'], p['w_pw1'], p['b_pw1'], p['w_dw'], p['b_dw'], p['mid_norm_w'], p['w_pw2'],
      p['b_pw2'], f['norm_w'], f['w_gate'], f['w_up'], f['w_down'])


def _rope_tables(seq):
    def cos_sin(half):
        inv = ROPE_BASE ** (-jnp.arange(half, dtype=F32) / half)
        ang = jnp.arange(seq, dtype=F32)[:, None] * inv[None, :]
        return jnp.cos(ang), jnp.sin(ang)

    c, s = cos_sin(RET_DIM // 2)
    cm, sm = cos_sin(MLA_ROPE // 2)
    z = jnp.zeros_like(cm)
    return {'cr': jnp.concatenate([c, c], axis=1), 'sr': jnp.concatenate([-s, s], axis=1),
            'cm': jnp.concatenate([cm, cm, z, z], axis=1), 'sm': jnp.concatenate([-sm, sm, z, z], axis=1)}


def _prep_hybrid(norm_w, w_in, decay_fwd, decay_bwd, gn_w, w_q_b, q_a_norm, w_kv_b, kv_a_norm, q_norm, k_norm,
                 w_out, chunk):
    kpe = w_in[:, -MLA_ROPE:]
    wq = w_q_b.reshape(MLA_Q_RANK, MLA_HEADS, MLA_QK)
    wq = jnp.concatenate([wq, wq[:, :, MLA_NOPE:]], axis=-1).reshape(MLA_Q_RANK, MLA_HEADS * MLA_QK_PAD)
    pad = lambda w: jnp.concatenate([w, jnp.zeros((MLA_QK_PAD - MLA_QK,), F32)])[None, :]
    bc = lambda d: jnp.broadcast_to(d[:, None, None], (RET_HEADS, 1, chunk))
    return {
        'hyb_norm_w': norm_w[None, :], 'w_in': jnp.concatenate([w_in, kpe], axis=1).astype(BF16),
        'decay_fwd': bc(decay_fwd), 'decay_bwd': bc(decay_bwd), 'gn_w': gn_w[:, None, :],
        'w_q_b': wq.astype(BF16), 'q_a_norm': q_a_norm[None, :], 'w_kv_b': w_kv_b.astype(BF16),
        'kv_a_norm': kv_a_norm[None, :], 'q_norm': pad(q_norm), 'k_norm': pad(k_norm),
        'w_out_ret': w_out[:RET_WIDTH].astype(BF16), 'w_out_att': w_out[RET_WIDTH:].astype(BF16),
    }


def _prep_conv(norm_w, w_pw1, b_pw1, w_dw, b_dw, mid_norm_w, w_pw2, b_pw2):
    return {'norm_w': norm_w[None, :], 'w_pw1': w_pw1.astype(BF16), 'b_pw1': b_pw1[None, :], 'w_dw': w_dw,
            'b_dw': b_dw[None, :], 'mid_norm_w': mid_norm_w[None, :], 'w_pw2': w_pw2.astype(BF16),
            'b_pw2': b_pw2[None, :]}


def _prep_ffn(norm_w, w_gate, w_up, w_down):
    return {'norm_w': norm_w[None, :], 'w_gate': w_gate.astype(BF16), 'w_up': w_up.astype(BF16),
            'w_down': w_down.astype(BF16)}


MAX_ATTN_UNROLL = 18


def _tiles(seq):
    tm = min(512, seq)
    tq, tk = min(256, seq), min(2048, seq)
    steady = (seq // tq) * (seq // tk) - 2
    unroll = max([r for r in range(2, MAX_ATTN_UNROLL + 1, 2) if steady % r == 0], default=2)
    return {'tm': tm, 'chunk': min(256, seq), 'tq': tq, 'tk': tk, 'attn_unroll': unroll}


def _run_group(x, hyb, conv, ffn0, ffn1, cfg):
    b, seq, _ = x.shape
    xf = x.reshape(b * seq, D_MODEL)
    tabs = _rope_tables(seq)
    qr, kr, vr, g, q, k, v = _hyb_in(xf, hyb, tabs, seq, cfg['tm'])
    r3 = lambda a: a.reshape(b, seq, a.shape[-1])
    ret = _retention(r3(qr), r3(kr), r3(vr), r3(g), hyb, cfg['chunk'])
    att = _attention(r3(q), r3(k), r3(v), cfg['tq'], cfg['tk'], cfg['attn_unroll'])
    x1 = _mix_ffn(xf, ret.reshape(b * seq, -1), att.reshape(b * seq, -1), hyb, ffn0, cfg['tm'])
    x2 = _conv_ffn(x1, conv, ffn1, seq, cfg['tm'])
    return x2.reshape(b, seq, D_MODEL)


def kernel(x_prompt, x_sample, hyb_norm_w, hyb_w_in, ret_decay_fwd, ret_decay_bwd, ret_gn_w, mla_w_q_b, mla_q_a_norm, mla_w_kv_b, mla_kv_a_norm, mla_q_norm, mla_k_norm, hyb_w_out, conv_norm_w, conv_w_pw1, conv_b_pw1, conv_w_dw, conv_b_dw, conv_mid_norm_w, conv_w_pw2, conv_b_pw2, ffn_norm_w, ffn_w_gate, ffn_w_up, ffn_w_down):
    conv = _prep_conv(conv_norm_w[0], conv_w_pw1[0], conv_b_pw1[0], conv_w_dw[0], conv_b_dw[0],
                      conv_mid_norm_w[0], conv_w_pw2[0], conv_b_pw2[0])
    ffn0 = _prep_ffn(ffn_norm_w[0], ffn_w_gate[0], ffn_w_up[0], ffn_w_down[0])
    ffn1 = _prep_ffn(ffn_norm_w[1], ffn_w_gate[1], ffn_w_up[1], ffn_w_down[1])

    def run(x):
        cfg = _tiles(x.shape[1])
        hyb = _prep_hybrid(hyb_norm_w[0], hyb_w_in[0], ret_decay_fwd[0], ret_decay_bwd[0], ret_gn_w[0],
                           mla_w_q_b[0], mla_q_a_norm[0], mla_w_kv_b[0], mla_kv_a_norm[0], mla_q_norm[0],
                           mla_k_norm[0], hyb_w_out[0], cfg['chunk'])
        return _run_group(x, hyb, conv, ffn0, ffn1, cfg)

    return (run(x_prompt), run(x_sample))
```
